```python
import math
import jax, jax.numpy as jnp
from jax import lax
import numpy as np

D_MODEL = 1024
BATCH = 32
SEQ = 256
DEPTH = 2
DEC_BATCH = 8
DEC_SEQ = 2048
PAST_LEN = 512

GRID_W = 64
D_A = D_MODEL
P_A = 64
H_A = D_A // P_A
G_A = 2
N_A = 128
CHUNK = 128
CONV_K = 4
XBC_A = D_A + 2 * G_A * N_A
D_B = D_MODEL
H_B = 8
BW_B = D_B // H_B
LRU_C = 8.0
H_C = 16
Q_RANK = 384
KV_RANK = 256
NOPE_C = 64
ROPE_C = 32
V_C = 64
N_FREQ = ROPE_C // 4
ROPE_BASE = 10000.0
Q_BLOCK = 128
D_FF = 2816
N_EXPERTS = 8
TOP_K = 2
D_FF_E = 3584
EPS = 1e-6
IN_EVEN = D_A + XBC_A + 2 * H_A + 2 * D_B
IN_ODD = Q_RANK + KV_RANK + ROPE_C

kernel_name = 'hybrid_ssd_rglru_mla_diffusion_step'


def rmsnorm(x, g):
    xf = x.astype(jnp.float32)
    y = xf * lax.rsqrt(jnp.mean(xf * xf, axis=-1, keepdims=True) + EPS)
    return (y * g.astype(jnp.float32)).astype(x.dtype)


def conv_centred(x, w, b):
    left = (CONV_K - 1) // 2
    y = lax.conv_general_dilated(x, w[:, None, :].astype(x.dtype), window_strides=(1,),
                                 padding=[(left, CONV_K - 1 - left)],
                                 dimension_numbers=('NWC', 'WIO', 'NWC'),
                                 feature_group_count=x.shape[-1])
    return y + b.astype(x.dtype)


def maybe_flip(t, d):
    return jnp.flip(t, axis=1) if d == 1 else t


def ssd_chunked(x, a, bm, cm, h0):
    b, n = x.shape[:2]
    c = n // CHUNK
    e = H_A // G_A
    x = x.reshape(b, c, CHUNK, G_A, e, P_A)
    a = a.reshape(b, c, CHUNK, G_A, e)
    bm = bm.reshape(b, c, CHUNK, G_A, N_A)
    cm = cm.reshape(b, c, CHUNK, G_A, N_A)
    a_cs = jnp.cumsum(a, axis=2)
    seg = a_cs[:, :, :, None] - a_cs[:, :, None, :]
    causal = jnp.tril(jnp.ones((CHUNK, CHUNK), dtype=bool))[None, None, :, :, None, None]
    decay = jnp.exp(jnp.where(causal, seg, -jnp.inf))
    cb = jnp.einsum('bclgn,bcsgn->bclsg', cm, bm)
    y_diag = jnp.einsum('bclsge,bcsgep->bclgep', cb[..., None] * decay, x)
    decay_s = jnp.exp(a_cs[:, :, -1:] - a_cs)
    states = jnp.einsum('bclgn,bclgep->bcgepn', bm, x * decay_s[..., None])
    chunk_decay = jnp.exp(a_cs[:, :, -1])

    def step(h, inp):
        d, s = inp
        return d[..., None, None] * h + s, h

    h_last, h_prev = lax.scan(step, h0.reshape(b, G_A, e, P_A, N_A),
                              (jnp.moveaxis(chunk_decay, 1, 0), jnp.moveaxis(states, 1, 0)))
    h_prev = jnp.moveaxis(h_prev, 0, 1)
    y_off = jnp.einsum('bclgn,bcgepn->bclgep', cm, h_prev) * jnp.exp(a_cs)[..., None]
    y = (y_diag + y_off).reshape(b, n, H_A, P_A)
    return y, h_last.reshape(b, H_A, P_A, N_A)


def linear_scan(log_a, u, h0):
    def combine(lhs, rhs):
        a1, u1 = lhs
        a2, u2 = rhs
        return a1 * a2, a2 * u1 + u2
    a_cum, h = lax.associative_scan(combine, (jnp.exp(log_a), u), axis=1)
    h = h + a_cum * h0[:, None, :]
    return h, h[:, -1]


def axial_rope_tables(n_tokens):
    rows = n_tokens // GRID_W
    row = jnp.repeat(jnp.arange(rows), GRID_W).astype(jnp.float32)
    col = jnp.tile(jnp.arange(GRID_W), rows).astype(jnp.float32)
    inv = ROPE_BASE ** (-jnp.arange(N_FREQ, dtype=jnp.float32) / N_FREQ)
    ang = jnp.stack([row[:, None] * inv, col[:, None] * inv], axis=1)
    return jnp.cos(ang), jnp.sin(ang)


def apply_axial_rope(x, cos, sin):
    if x.ndim == 4:
        cos, sin = cos[:, None], sin[:, None]
    xr = x.astype(jnp.float32).reshape(x.shape[:-1] + (2, 2, N_FREQ))
    x1, x2 = xr[..., 0, :], xr[..., 1, :]
    o = jnp.stack([x1 * cos - x2 * sin, x2 * cos + x1 * sin], axis=-2)
    return o.reshape(x.shape).astype(x.dtype)


def block_attention(q_nope, q_rope, k_nope, k_rope, v):
    b, n = q_nope.shape[:2]
    nb = n // Q_BLOCK
    scale = (NOPE_C + ROPE_C) ** -0.5
    qn = q_nope.reshape(b, nb, Q_BLOCK, H_C, NOPE_C).swapaxes(0, 1)
    qr = q_rope.reshape(b, nb, Q_BLOCK, H_C, ROPE_C).swapaxes(0, 1)

    def one_block(args):
        qnb, qrb = args
        s = jnp.einsum('bqhd,bkhd->bhqk', qnb, k_nope) + jnp.einsum('bqhd,bkd->bhqk', qrb, k_rope)
        p = jax.nn.softmax(s.astype(jnp.float32) * scale, axis=-1).astype(v.dtype)
        return jnp.einsum('bhqk,bkhd->bqhd', p, v)

    o = lax.map(one_block, (qn, qr))
    return o.swapaxes(0, 1).reshape(b, n, H_C, V_C)


def even_mixer(h, s_ssd0, s_lru0, P, j):
    f32 = jnp.float32
    b, n, _ = h.shape
    proj = h @ P['ev_w_in'][j]
    z, xbc, dt_raw, gate, xb = jnp.split(
        proj, [D_A, D_A + XBC_A, D_A + XBC_A + 2 * H_A, D_A + XBC_A + 2 * H_A + D_B], axis=-1)
    xbc = jax.nn.silu(conv_centred(xbc, P['ev_conv_a_w'][j], P['ev_conv_a_b'][j])).astype(f32)
    xs, bm, cm = jnp.split(xbc, [D_A, D_A + G_A * N_A], axis=-1)
    xs = xs.reshape(b, n, H_A, P_A)
    bm = bm.reshape(b, n, G_A, N_A)
    cm = cm.reshape(b, n, G_A, N_A)
    dt = jax.nn.softplus(dt_raw.astype(f32).reshape(b, n, 2, H_A) + P['ev_dt_bias'][j].astype(f32))
    a_rate = -jnp.exp(P['ev_a_log'][j].astype(f32))
    y = P['ev_d_skip'][j].astype(f32)[:, None] * xs
    ssd_final = []
    for d in range(2):
        y_d, s_d = ssd_chunked(maybe_flip(xs * dt[:, :, d, :, None], d),
                               maybe_flip(dt[:, :, d] * a_rate[d], d),
                               maybe_flip(bm, d), maybe_flip(cm, d),
                               s_ssd0[:, d].astype(f32))
        y = y + maybe_flip(y_d, d)
        ssd_final.append(s_d)
    y = (y * jax.nn.silu(z.astype(f32)).reshape(b, n, H_A, P_A)).reshape(b, n, G_A, D_A // G_A)
    y_a = rmsnorm(y, P['ev_gnorm'][j].reshape(G_A, D_A // G_A)).reshape(b, n, D_A)
    xr = conv_centred(xb, P['ev_conv_b_w'][j], P['ev_conv_b_b'][j]).astype(f32)
    gates = jnp.einsum('bnhi,dkhij->bndkhj', xr.reshape(b, n, H_B, BW_B), P['ev_lru_w'][j].astype(f32))
    gates = gates.reshape(b, n, 2, 2, D_B) + P['ev_lru_b'][j].astype(f32)
    r_gate = jax.nn.sigmoid(gates[:, :, :, 0])
    i_gate = jax.nn.sigmoid(gates[:, :, :, 1])
    log_a = -LRU_C * r_gate * jax.nn.softplus(-P['ev_lru_lam'][j].astype(f32))
    u = jnp.sqrt(jnp.maximum(-jnp.expm1(2.0 * log_a), 0.0)) * (i_gate * xr[:, :, None, :])
    hs = []
    lru_final = []
    for d in range(2):
        h_seq, h_fin = linear_scan(maybe_flip(log_a[:, :, d], d), maybe_flip(u[:, :, d], d),
                                   s_lru0[:, d].astype(f32))
        hs.append(maybe_flip(h_seq, d))
        lru_final.append(h_fin)
    y_b = jax.nn.gelu(gate.astype(f32)) * (hs[0] + hs[1])
    out = jnp.concatenate([y_a, y_b], axis=-1).astype(h.dtype) @ P['ev_w_out'][j]
    return out, jnp.stack(ssd_final, axis=1), jnp.stack(lru_final, axis=1)


def odd_mixer(h, rope, ctx, P, j):
    b, n, _ = h.shape
    proj = h @ P['od_w_in'][j]
    cq, ckv, kr = jnp.split(proj, [Q_RANK, Q_RANK + KV_RANK], axis=-1)
    q = (rmsnorm(cq, P['od_q_norm'][j]) @ P['od_w_q_up'][j]).reshape(b, n, H_C, NOPE_C + ROPE_C)
    q_nope, q_rope = q[..., :NOPE_C], q[..., NOPE_C:]
    ckv = rmsnorm(ckv, P['od_kv_norm'][j])
    if rope is not None:
        cos, sin = rope
        q_rope = apply_axial_rope(q_rope, cos, sin)
        kr = apply_axial_rope(kr, cos, sin)
    if ctx is None:
        all_ckv, all_kr = ckv, kr
    else:
        all_ckv = jnp.concatenate([ctx[0].astype(ckv.dtype), ckv], axis=1)
        all_kr = jnp.concatenate([ctx[1].astype(kr.dtype), kr], axis=1)
    kv = (all_ckv @ P['od_w_kv_up'][j]).reshape(b, all_ckv.shape[1], H_C, NOPE_C + V_C)
    o = block_attention(q_nope, q_rope, kv[..., :NOPE_C], all_kr, kv[..., NOPE_C:])
    out = o.reshape(b, n, H_C * V_C) @ P['od_w_o'][j]
    return out, ckv, kr


def swiglu(h, w_gu, w_down):
    g, u = jnp.split(h @ w_gu, 2, axis=-1)
    return (jax.nn.silu(g) * u) @ w_down


def moe(h, w_router, w_gu, w_down):
    logits = (h @ w_router).astype(jnp.float32)
    top_v, top_i = lax.top_k(logits, TOP_K)
    wts = jax.nn.softmax(top_v, axis=-1)
    gate = jnp.einsum('bnk,bnke->bne', wts,
                      jax.nn.one_hot(top_i, N_EXPERTS, dtype=jnp.float32)).astype(h.dtype)
    out = gate[..., 0:1] * swiglu(h, w_gu[0], w_down[0])
    for e in range(1, N_EXPERTS):
        out = out + gate[..., e:e + 1] * swiglu(h, w_gu[e], w_down[e])
    return out


def trunk(x, cond, rope, ctx, P):
    b = x.shape[0]
    new_ssd, new_lru, new_ckv, new_kr = [], [], [], []
    for l in range(DEPTH):
        j = l // 2
        mod = jax.nn.silu(cond) @ P['mod_w'][l] + P['mod_b'][l]
        sh1, sc1, g1, sh2, sc2, g2 = jnp.split(mod[:, None, :], 6, axis=-1)
        h = rmsnorm(x, P['norm_g'][l, 0]) * (1.0 + sc1) + sh1
        if l % 2 == 0:
            if ctx is None:
                s0 = jnp.zeros((b, 2, H_A, P_A, N_A), jnp.float32)
                r0 = jnp.zeros((b, 2, D_B), jnp.float32)
            else:
                s0, r0 = ctx[0][:, j], ctx[1][:, j]
            mix, s_fin, r_fin = even_mixer(h, s0, r0, P, j)
            if ctx is None:
                new_ssd.append(s_fin)
                new_lru.append(r_fin)
        else:
            c_kv = None if ctx is None else (ctx[2][:, j], ctx[3][:, j])
            mix, ckv, kr = odd_mixer(h, rope, c_kv, P, j)
            if ctx is None:
                new_ckv.append(ckv)
                new_kr.append(kr)
        x = x + g1 * mix
        h = rmsnorm(x, P['norm_g'][l, 1]) * (1.0 + sc2) + sh2
        if l % 2 == 0:
            f = swiglu(h, P['ev_ffn_gu'][j], P['ev_ffn_down'][j])
        else:
            f = moe(h, P['od_router'][j], P['od_moe_gu'][j], P['od_moe_down'][j])
        x = x + g2 * f
    y = rmsnorm(x, P['final_g'])
    if ctx is None:
        return y, (jnp.stack(new_ssd, axis=1), jnp.stack(new_lru, axis=1),
                   jnp.stack(new_ckv, axis=1), jnp.stack(new_kr, axis=1))
    return y, None


def setup_inputs(seed: int = 0) -> dict:
    key = jax.random.key(seed)
    ks = iter(jax.random.split(key, 40))
    f32 = jnp.float32
    ne = (DEPTH + 1) // 2
    no = DEPTH // 2

    def nrm(shape, scale):
        return jax.random.normal(next(ks), shape, f32) * scale

    def gain(shape):
        return 1.0 + 0.02 * jax.random.normal(next(ks), shape, f32)

    def unif(shape, lo, hi):
        return jax.random.uniform(next(ks), shape, f32, lo, hi)

    x_prompt = nrm((BATCH, SEQ, D_MODEL), 1.0)
    x_sample = nrm((DEC_BATCH, DEC_SEQ, D_MODEL), 1.0)
    state_ssd = nrm((DEC_BATCH, ne, 2, H_A, P_A, N_A), 0.1)
    state_lru = nrm((DEC_BATCH, ne, 2, D_B), 0.5)
    cache_ckv = nrm((DEC_BATCH, no, PAST_LEN, KV_RANK), 1.0)
    cache_krope = nrm((DEC_BATCH, no, PAST_LEN, ROPE_C), 1.0)
    c = nrm((DEC_BATCH, D_MODEL), 1.0)
    c_ctx = nrm((D_MODEL,), 1.0)
    mod_w = nrm((DEPTH, D_MODEL, 6 * D_MODEL), 0.3 * D_MODEL ** -0.5)
    mod_b = nrm((DEPTH, 6 * D_MODEL), 0.02)
    norm_g = gain((DEPTH, 2, D_MODEL))
    final_g = gain((D_MODEL,))
    ev_w_in = nrm((ne, D_MODEL, IN_EVEN), D_MODEL ** -0.5)
    ev_conv_a_w = nrm((ne, CONV_K, XBC_A), CONV_K ** -0.5)
    ev_conv_a_b = nrm((ne, XBC_A), 0.02)
    ev_a_log = jnp.log(unif((ne, 2, H_A), 1.0, 16.0))
    dt0 = jnp.exp(unif((ne, 2, H_A), math.log(1e-3), math.log(1e-1)))
    ev_dt_bias = dt0 + jnp.log(-jnp.expm1(-dt0))
    ev_d_skip = gain((ne, H_A))
    ev_gnorm = gain((ne, D_A))
    ev_conv_b_w = nrm((ne, CONV_K, D_B), CONV_K ** -0.5)
    ev_conv_b_b = nrm((ne, D_B), 0.02)
    ev_lru_w = nrm((ne, 2, 2, H_B, BW_B, BW_B), BW_B ** -0.5)
    ev_lru_b = nrm((ne, 2, 2, D_B), 0.02)
    a_sq = unif((ne, 2, D_B), 0.9, 0.999) ** (1.0 / LRU_C)
    ev_lru_lam = jnp.log(a_sq) - jnp.log1p(-a_sq)
    ev_w_out = nrm((ne, D_A + D_B, D_MODEL), (D_A + D_B) ** -0.5)
    ev_ffn_gu = nrm((ne, D_MODEL, 2 * D_FF), D_MODEL ** -0.5)
    ev_ffn_down = nrm((ne, D_FF, D_MODEL), D_FF ** -0.5)
    od_w_in = nrm((no, D_MODEL, IN_ODD), D_MODEL ** -0.5)
    od_q_norm = gain((no, Q_RANK))
    od_w_q_up = nrm((no, Q_RANK, H_C * (NOPE_C + ROPE_C)), Q_RANK ** -0.5)
    od_kv_norm = gain((no, KV_RANK))
    od_w_kv_up = nrm((no, KV_RANK, H_C * (NOPE_C + V_C)), KV_RANK ** -0.5)
    od_w_o = nrm((no, H_C * V_C, D_MODEL), (H_C * V_C) ** -0.5)
    od_router = nrm((no, D_MODEL, N_EXPERTS), D_MODEL ** -0.5)
    od_moe_gu = nrm((no, N_EXPERTS, D_MODEL, 2 * D_FF_E), D_MODEL ** -0.5)
    od_moe_down = nrm((no, N_EXPERTS, D_FF_E, D_MODEL), D_FF_E ** -0.5)
    return {
        'x_prompt': x_prompt, 'x_sample': x_sample,
        'state_ssd': state_ssd, 'state_lru': state_lru,
        'cache_ckv': cache_ckv, 'cache_krope': cache_krope,
        'c': c, 'c_ctx': c_ctx,
        'mod_w': mod_w, 'mod_b': mod_b, 'norm_g': norm_g, 'final_g': final_g,
        'ev_w_in': ev_w_in, 'ev_conv_a_w': ev_conv_a_w, 'ev_conv_a_b': ev_conv_a_b,
        'ev_a_log': ev_a_log, 'ev_dt_bias': ev_dt_bias, 'ev_d_skip': ev_d_skip, 'ev_gnorm': ev_gnorm,
        'ev_conv_b_w': ev_conv_b_w, 'ev_conv_b_b': ev_conv_b_b,
        'ev_lru_w': ev_lru_w, 'ev_lru_b': ev_lru_b, 'ev_lru_lam': ev_lru_lam,
        'ev_w_out': ev_w_out, 'ev_ffn_gu': ev_ffn_gu, 'ev_ffn_down': ev_ffn_down,
        'od_w_in': od_w_in, 'od_q_norm': od_q_norm, 'od_w_q_up': od_w_q_up,
        'od_kv_norm': od_kv_norm, 'od_w_kv_up': od_w_kv_up, 'od_w_o': od_w_o,
        'od_router': od_router, 'od_moe_gu': od_moe_gu, 'od_moe_down': od_moe_down,
    }


def reference(x_prompt, x_sample, state_ssd, state_lru, cache_ckv, cache_krope, c, c_ctx,
              mod_w, mod_b, norm_g, final_g,
              ev_w_in, ev_conv_a_w, ev_conv_a_b, ev_a_log, ev_dt_bias, ev_d_skip, ev_gnorm,
              ev_conv_b_w, ev_conv_b_b, ev_lru_w, ev_lru_b, ev_lru_lam, ev_w_out, ev_ffn_gu, ev_ffn_down,
              od_w_in, od_q_norm, od_w_q_up, od_kv_norm, od_w_kv_up, od_w_o,
              od_router, od_moe_gu, od_moe_down):
    P = {
        'mod_w': mod_w, 'mod_b': mod_b, 'norm_g': norm_g, 'final_g': final_g,
        'ev_w_in': ev_w_in, 'ev_conv_a_w': ev_conv_a_w, 'ev_conv_a_b': ev_conv_a_b,
        'ev_a_log': ev_a_log, 'ev_dt_bias': ev_dt_bias, 'ev_d_skip': ev_d_skip, 'ev_gnorm': ev_gnorm,
        'ev_conv_b_w': ev_conv_b_w, 'ev_conv_b_b': ev_conv_b_b,
        'ev_lru_w': ev_lru_w, 'ev_lru_b': ev_lru_b, 'ev_lru_lam': ev_lru_lam,
        'ev_w_out': ev_w_out, 'ev_ffn_gu': ev_ffn_gu, 'ev_ffn_down': ev_ffn_down,
        'od_w_in': od_w_in, 'od_q_norm': od_q_norm, 'od_w_q_up': od_w_q_up,
        'od_kv_norm': od_kv_norm, 'od_w_kv_up': od_w_kv_up, 'od_w_o': od_w_o,
        'od_router': od_router, 'od_moe_gu': od_moe_gu, 'od_moe_down': od_moe_down,
    }
    y_prompt, ctx_states = trunk(x_prompt, c_ctx[None, :], None, None, P)
    new_state_ssd, new_state_lru, new_cache_ckv, new_cache_krope = ctx_states
    rope = axial_rope_tables(x_sample.shape[1])
    y_sample, _ = trunk(x_sample, c, rope, (state_ssd, state_lru, cache_ckv, cache_krope), P)
    return (y_prompt, y_sample, new_state_ssd, new_state_lru, new_cache_ckv, new_cache_krope)
```

```python
import functools
import math

import jax
import jax.numpy as jnp
from jax import lax
from jax.experimental import pallas as pl
from jax.experimental.pallas import tpu as pltpu

F32 = jnp.float32
BF16 = jnp.bfloat16
HIGHEST = lax.Precision.HIGHEST

D_MODEL = 1024
GRID_W = 64
P_A = 64
H_A = 16
G_A = 2
N_A = 128
CHUNK = 128
CONV_K = 4
H_B = 8
BW_B = 128
LRU_C = 8.0
H_C = 16
Q_RANK = 384
KV_RANK = 256
NOPE_C = 64
ROPE_C = 32
V_C = 64
N_FREQ = ROPE_C // 4
ROPE_BASE = 10000.0
D_FF = 2816
N_EXPERTS = 8
D_FF_E = 3584
EPS = 1e-6

LANES = 128
HEAD_PAD = 128
ROW_BLOCK = 2048
ROW_CHUNK = 256
SEG_PITCH = ROW_CHUNK + 8
TM = 512
TM_E = 1024
FF_CHUNK_E = 512
ROUTE_BLOCK = 1024
ROUTE_BATCH = 32
VMEM_LIMIT = 52 * 1024 * 1024


def _cparams(n_axes, vmem=VMEM_LIMIT):
    return pltpu.CompilerParams(dimension_semantics=("arbitrary",) * n_axes, vmem_limit_bytes=vmem)


def _dot(a, b):
    return jnp.dot(a, b, preferred_element_type=F32)


def _sigmoid(x):
    return 1.0 / (1.0 + jnp.exp(-x))


def _silu(x):
    return x * _sigmoid(x)


def _softplus(x):
    return jnp.maximum(x, 0.0) + jnp.log(1.0 + jnp.exp(-jnp.abs(x)))


def _gelu_tanh(x):
    return 0.5 * x * (1.0 + jnp.tanh(math.sqrt(2.0 / math.pi) * (x + 0.044715 * (x * x * x))))


def _modulated_norm(x, gain, scale, shift):
    ms = jnp.mean(x * x, axis=-1, keepdims=True)
    return (x * lax.rsqrt(ms + EPS)) * (gain * (1.0 + scale)) + shift


def _const_spec(shape):
    zeros = (0,) * len(shape)
    return pl.BlockSpec(shape, lambda *_: zeros, pipeline_mode=pl.Buffered(1))


def _mod_kernel(c_ref, w_ref, b_ref, o_ref):
    c = c_ref[...]
    o_ref[0] = _dot(_silu(c).astype(BF16), w_ref[0].astype(BF16)) + b_ref[0]


def _modulation(cond, mod_w, mod_b):
    depth = mod_w.shape[0]
    n = 6 * D_MODEL
    tn = n // 4
    out = pl.pallas_call(
        _mod_kernel,
        out_shape=jax.ShapeDtypeStruct((depth, 16, n), F32),
        grid=(depth, 4),
        in_specs=[pl.BlockSpec((16, D_MODEL), lambda l, j: (0, 0)),
                  pl.BlockSpec((1, D_MODEL, tn), lambda l, j: (l, 0, j)),
                  pl.BlockSpec((1, 1, tn), lambda l, j: (l, 0, j))],
        out_specs=pl.BlockSpec((1, 16, tn), lambda l, j: (l, 0, j)),
        compiler_params=_cparams(2),
        name="modulation",
    )(cond, mod_w, mod_b.reshape(depth, 1, n))
    return out.reshape(depth, 16, 6, D_MODEL)


def _ev_inproj_kernel(x_ref, mod_ref, g_ref, w_ref, wdt_ref, cw_ref, cb_ref, dtb_ref,
                      p_ref, dt_ref, h_scr, acc_scr, *, seq_is_chunk):
    j = pl.program_id(1)
    rows = h_scr.shape[0]
    n_chunks = rows // ROW_CHUNK
    width = w_ref.shape[1]

    @pl.when(j == 0)
    def _():
        gain, scale, shift = g_ref[...], mod_ref[0, 1:2, :], mod_ref[0, 0:1, :]

        def body(r, carry):
            r0 = pl.multiple_of(r * ROW_CHUNK, ROW_CHUNK)
            h = _modulated_norm(x_ref[pl.ds(r0, ROW_CHUNK), :], gain, scale, shift)
            h_scr[pl.ds(r0, ROW_CHUNK), :] = h.astype(BF16)
            return carry

        lax.fori_loop(0, n_chunks, body, 0)
        dt = _softplus(_dot(h_scr[...], wdt_ref[...]) + dtb_ref[...])
        lane = lax.broadcasted_iota(jnp.int32, dt.shape, 1)
        dt_ref[...] = jnp.where(lane < 2 * H_A, dt, 0.0)
        acc_scr[0:8, :] = jnp.zeros((8, width), F32)
        acc_scr[rows + 8:rows + 16, :] = jnp.zeros((8, width), F32)

    acc_scr[8:rows + 8, :] = _dot(h_scr[...], w_ref[...])

    def plain(act):
        def body(r, carry):
            r0 = pl.multiple_of(r * ROW_CHUNK, ROW_CHUNK)
            v = acc_scr[pl.ds(r0 + 8, ROW_CHUNK), :]
            p_ref[pl.ds(r0, ROW_CHUNK), :] = act(v).astype(BF16)
            return carry
        lax.fori_loop(0, n_chunks, body, 0)

    def conv(act):
        win = ROW_CHUNK + 16
        cw = cw_ref[...]
        cb = cb_ref[...]

        def body(r, carry):
            r0 = pl.multiple_of(r * ROW_CHUNK, ROW_CHUNK)
            xw = acc_scr[pl.ds(r0, win), :]
            if seq_is_chunk:
                row = lax.broadcasted_iota(jnp.int32, xw.shape, 0)
                xw = jnp.where((row >= 8) & (row < ROW_CHUNK + 8), xw, 0.0)
            y = cb + cw[1:2, :] * xw[8:ROW_CHUNK + 8]
            for k in (0, 2, 3):
                sh = pltpu.roll(xw, (1 - k) % win, 0)[8:ROW_CHUNK + 8]
                y = y + cw[k:k + 1, :] * sh
            p_ref[pl.ds(r0, ROW_CHUNK), :] = act(y).astype(BF16)
            return carry
        lax.fori_loop(0, n_chunks, body, 0)

    @pl.when(j < 2)
    def _():
        plain(lambda v: v)

    @pl.when((j >= 2) & (j < 4))
    def _():
        conv(lambda v: v)

    @pl.when((j >= 4) & (j < 6))
    def _():
        plain(_gelu_tanh)

    @pl.when(j >= 6)
    def _():
        conv(_silu)


P_COLS = 4608


def _ev_inproj(x2d, mod_l, cond_of_block, gain, w, wdt, cw, cb, dtb, seq_len):
    n = x2d.shape[0]
    nblk = n // ROW_BLOCK
    tn = 512
    kern = functools.partial(_ev_inproj_kernel, seq_is_chunk=(seq_len == ROW_CHUNK))
    return pl.pallas_call(
        kern,
        out_shape=(jax.ShapeDtypeStruct((n, P_COLS), BF16), jax.ShapeDtypeStruct((n, LANES), F32)),
        grid=(nblk, P_COLS // tn),
        in_specs=[pl.BlockSpec((ROW_BLOCK, D_MODEL), lambda i, j: (i, 0)),
                  pl.BlockSpec((1, 6, D_MODEL), lambda i, j: (cond_of_block(i), 0, 0)),
                  pl.BlockSpec((1, D_MODEL), lambda i, j: (0, 0)),
                  pl.BlockSpec((D_MODEL, tn), lambda i, j: (0, j)),
                  pl.BlockSpec((D_MODEL, LANES), lambda i, j: (0, 0)),
                  pl.BlockSpec((CONV_K, tn), lambda i, j: (0, j)),
                  pl.BlockSpec((1, tn), lambda i, j: (0, j)),
                  pl.BlockSpec((1, LANES), lambda i, j: (0, 0))],
        out_specs=(pl.BlockSpec((ROW_BLOCK, tn), lambda i, j: (i, j)),
                   pl.BlockSpec((ROW_BLOCK, LANES), lambda i, j: (i, 0))),
        scratch_shapes=[pltpu.VMEM((ROW_BLOCK, D_MODEL), BF16),
                        pltpu.VMEM((ROW_BLOCK + 16, tn), F32)],
        compiler_params=_cparams(2),
        name="ev_inproj",
    )(x2d, mod_l, gain, w, wdt, cw, cb, dtb)


def _ssd_direction(d, xs_ref, bm_ref, cm_ref, dt_ref, arate, e2, h_scr, y_ref):
    xs = xs_ref[...].astype(F32)
    bm = bm_ref[...]
    cm = cm_ref[...]
    a = dt_ref[...] * arate
    li = lax.broadcasted_iota(jnp.int32, (CHUNK, CHUNK), 0)
    si = lax.broadcasted_iota(jnp.int32, (CHUNK, CHUNK), 1)
    keep = (si <= li) if d == 0 else (si >= li)
    tri = jnp.where(keep, 1.0, 0.0).astype(F32)
    acs = jnp.dot(tri, a, precision=HIGHEST, preferred_element_type=F32)
    acs_t = acs.T
    tot = acs[CHUNK - 1:CHUNK, :] if d == 0 else acs[0:1, :]
    stack = jnp.concatenate([dt_ref[...], jnp.exp(acs), jnp.exp(tot - acs)], axis=0)
    hi = stack.astype(BF16)
    lo = (stack - hi.astype(F32)).astype(BF16)
    ex = _dot(jnp.concatenate([hi, lo], axis=1), e2)
    dt_e, eacs_e, ds_e = ex[0:CHUNK], ex[CHUNK:2 * CHUNK], ex[2 * CHUNK:3 * CHUNK]
    xdt = xs * dt_e
    xdt_b = xdt.astype(BF16)
    xds_b = (xdt * ds_e).astype(BF16)
    lane = lax.broadcasted_iota(jnp.int32, (CHUNK, LANES), 1)
    gw = (H_A // G_A) * P_A
    y_groups = []
    for g in range(G_A):
        bm_g = bm[:, g * N_A:(g + 1) * N_A]
        cm_g = cm[:, g * N_A:(g + 1) * N_A]
        cb = lax.dot_general(cm_g, bm_g, (((1,), (1,)), ((), ())), preferred_element_type=F32)
        h_prev = h_scr[d, :, g * gw:(g + 1) * gw]
        y_off = _dot(cm_g, h_prev.astype(BF16)) * eacs_e[:, g * gw:(g + 1) * gw]
        bm_t = bm_g.astype(F32).T.astype(BF16)
        st = _dot(bm_t, xds_b[:, g * gw:(g + 1) * gw])
        decay_tot = eacs_e[CHUNK - 1:CHUNK, g * gw:(g + 1) * gw] if d == 0 else eacs_e[0:1, g * gw:(g + 1) * gw]
        h_scr[d, :, g * gw:(g + 1) * gw] = decay_tot * h_prev + st
        pairs = []
        for q in range(H_A // G_A // 2):
            ms = []
            for e in (g * 8 + 2 * q, g * 8 + 2 * q + 1):
                col = acs[:, d * H_A + e:d * H_A + e + 1]
                row = acs_t[d * H_A + e:d * H_A + e + 1, :]
                dec = jnp.where(keep, jnp.exp(jnp.minimum(col - row, 0.0)), 0.0)
                ms.append((cb * dec).astype(BF16))
            lhs = jnp.concatenate(ms, axis=1)
            xp = xdt_b[:, (g * 8 + 2 * q) * P_A:(g * 8 + 2 * q + 2) * P_A]
            zero = jnp.zeros_like(xp)
            rhs = jnp.concatenate([jnp.where(lane < P_A, xp, zero), jnp.where(lane >= P_A, xp, zero)], axis=0)
            pairs.append(_dot(lhs, rhs))
        y_groups.append(jnp.concatenate(pairs, axis=1) + y_off)
    y_ref[...] = jnp.concatenate(y_groups, axis=1)


def _ssd_kernel(*refs, has_h0, emit_final):
    (xsf, bmf, cmf, dtf, xsb, bmb, cmb, dtb, alog_ref, e2_ref), rest = refs[:10], refs[10:]
    if has_h0:
        h0_ref, rest = rest[0], rest[1:]
    yf_ref, yb_ref, rest = rest[0], rest[1], rest[2:]
    if emit_final:
        hfin_ref, rest = rest[0], rest[1:]
    h_scr = rest[0]
    i = pl.program_id(1)

    @pl.when(i == 0)
    def _():
        for d in range(2):
            if has_h0:
                h_scr[d] = h0_ref[0, d].T
            else:
                h_scr[d] = jnp.zeros(h_scr.shape[1:], F32)

    arate = -jnp.exp(alog_ref[...])
    _ssd_direction(0, xsf, bmf, cmf, dtf, arate, e2_ref[0], h_scr, yf_ref)
    _ssd_direction(1, xsb, bmb, cmb, dtb, arate, e2_ref[1], h_scr, yb_ref)

    if emit_final:
        @pl.when(i == pl.num_programs(1) - 1)
        def _():
            for d in range(2):
                hfin_ref[0, d] = h_scr[d].T


def _ssd(p, dt, alog_row, e2, h0, n_batch, seq_len, emit_final):
    n = p.shape[0]
    nc = seq_len // CHUNK
    hp = H_A * P_A
    fwd = lambda b, i: b * nc + i
    bwd = lambda b, i: b * nc + (nc - 1 - i)
    xs_blk, bm_blk, cm_blk = 3, 16, 17

    def specs(rowf):
        return [pl.BlockSpec((CHUNK, hp), lambda b, i: (rowf(b, i), xs_blk)),
                pl.BlockSpec((CHUNK, G_A * N_A), lambda b, i: (rowf(b, i), bm_blk)),
                pl.BlockSpec((CHUNK, G_A * N_A), lambda b, i: (rowf(b, i), cm_blk)),
                pl.BlockSpec((CHUNK, LANES), lambda b, i: (rowf(b, i), 0))]

    in_specs = specs(fwd) + specs(bwd) + [pl.BlockSpec((1, LANES), lambda b, i: (0, 0)),
                                          pl.BlockSpec((2, 2 * LANES, hp), lambda b, i: (0, 0, 0))]
    args = [p, p, p, dt, p, p, p, dt, alog_row, e2]
    if h0 is not None:
        in_specs.append(pl.BlockSpec((1, 2, hp, N_A), lambda b, i: (b, 0, 0, 0)))
        args.append(h0)
    out_shape = [jax.ShapeDtypeStruct((n, hp), F32), jax.ShapeDtypeStruct((n, hp), F32)]
    out_specs = [pl.BlockSpec((CHUNK, hp), lambda b, i: (fwd(b, i), 0)),
                 pl.BlockSpec((CHUNK, hp), lambda b, i: (bwd(b, i), 0))]
    if emit_final:
        out_shape.append(jax.ShapeDtypeStruct((n_batch, 2, hp, N_A), F32))
        out_specs.append(pl.BlockSpec((1, 2, hp, N_A), lambda b, i: (b, 0, 0, 0)))
    kern = functools.partial(_ssd_kernel, has_h0=h0 is not None, emit_final=emit_final)
    return pl.pallas_call(
        kern, out_shape=tuple(out_shape), grid=(n_batch, nc),
        in_specs=in_specs, out_specs=tuple(out_specs),
        scratch_shapes=[pltpu.VMEM((2, N_A, hp), F32)],
        compiler_params=_cparams(2),
        name="ssd",
    )(*args)


def _lru_kernel(*refs, chain, emit_final):
    (xr_ref, gg_ref, wl_ref, lb_ref, lam_ref), rest = refs[:5], refs[5:]
    if chain:
        h0_ref, rest = rest[0], rest[1:]
    y_ref, rest = rest[0], rest[1:]
    if emit_final:
        fin_ref, rest = rest[0], rest[1:]
    a_scr, u_scr, h_scr, p_scr = rest
    rows = xr_ref.shape[0]
    n_seg = rows // ROW_CHUNK
    sp = _softplus(-lam_ref[0])

    def gates(r, carry):
        r0 = pl.multiple_of(r * ROW_CHUNK, ROW_CHUNK)
        s0 = pl.multiple_of(r * SEG_PITCH, 8)
        xr_b = xr_ref[pl.ds(r0, ROW_CHUNK), :]
        xr = xr_b.astype(F32)
        g = _dot(xr_b, wl_ref[0]) + lb_ref[0]
        for d in range(2):
            r_gate = _sigmoid(g[:, (2 * d) * BW_B:(2 * d + 1) * BW_B])
            i_gate = _sigmoid(g[:, (2 * d + 1) * BW_B:(2 * d + 2) * BW_B])
            a = jnp.exp((-LRU_C) * r_gate * sp[d:d + 1, :])
            u = jnp.sqrt(jnp.maximum(1.0 - a * a, 0.0)) * (i_gate * xr)
            a_scr[d, pl.ds(s0, ROW_CHUNK), :] = a
            u_scr[d, pl.ds(s0, ROW_CHUNK), :] = u
        return carry

    lax.fori_loop(0, n_seg, gates, 0)

    def step(t, carry):
        hf, hb, pf, pb = carry
        tb = ROW_CHUNK - 1 - t
        sf = pl.ds(t, n_seg, stride=SEG_PITCH)
        sb = pl.ds(tb, n_seg, stride=SEG_PITCH)
        af = a_scr[0, sf, :]
        ab = a_scr[1, sb, :]
        hf = af * hf + u_scr[0, sf, :]
        hb = ab * hb + u_scr[1, sb, :]
        h_scr[0, sf, :] = hf
        h_scr[1, sb, :] = hb
        if chain:
            pf = af * pf
            pb = ab * pb
            p_scr[0, sf, :] = pf
            p_scr[1, sb, :] = pb
        return hf, hb, pf, pb

    z = jnp.zeros((n_seg, BW_B), F32)
    o = jnp.ones((n_seg, BW_B), F32)
    lax.fori_loop(0, ROW_CHUNK, step, (z, z, o, o), unroll=8)

    if emit_final:
        fin_ref[0, 0, 0] = h_scr[0, pl.ds(ROW_CHUNK - 1, n_seg, stride=SEG_PITCH), :]
        fin_ref[0, 0, 1] = h_scr[1, pl.ds(0, n_seg, stride=SEG_PITCH), :]

    if chain:
        carry = h0_ref[0, 0, 0:1, :]
        for s in range(n_seg):
            sl = slice(s * SEG_PITCH, s * SEG_PITCH + ROW_CHUNK)
            h = h_scr[0, sl, :] + p_scr[0, sl, :] * carry
            h_scr[0, sl, :] = h
            carry = h[ROW_CHUNK - 1:ROW_CHUNK, :]
        carry = h0_ref[0, 0, 1:2, :]
        for s in reversed(range(n_seg)):
            sl = slice(s * SEG_PITCH, s * SEG_PITCH + ROW_CHUNK)
            h = h_scr[1, sl, :] + p_scr[1, sl, :] * carry
            h_scr[1, sl, :] = h
            carry = h[0:1, :]

    for s in range(n_seg):
        sl = slice(s * SEG_PITCH, s * SEG_PITCH + ROW_CHUNK)
        gg = gg_ref[s * ROW_CHUNK:(s + 1) * ROW_CHUNK, :].astype(F32)
        y_ref[s * ROW_CHUNK:(s + 1) * ROW_CHUNK, :] = (gg * (h_scr[0, sl, :] + h_scr[1, sl, :])).astype(BF16)


def _lru(p, wl, lb, lam, h0, chain, emit_final):
    n = p.shape[0]
    nblk = n // ROW_BLOCK
    n_seg = ROW_BLOCK // ROW_CHUNK
    xb_blk, gate_blk = D_MODEL // BW_B, 2 * D_MODEL // BW_B
    in_specs = [pl.BlockSpec((ROW_BLOCK, BW_B), lambda i, h: (i, xb_blk + h)),
                pl.BlockSpec((ROW_BLOCK, BW_B), lambda i, h: (i, gate_blk + h)),
                pl.BlockSpec((1, BW_B, 4 * BW_B), lambda i, h: (h, 0, 0)),
                pl.BlockSpec((1, 1, 4 * BW_B), lambda i, h: (h, 0, 0)),
                pl.BlockSpec((1, 2, BW_B), lambda i, h: (h, 0, 0))]
    args = [p, p, wl, lb, lam]
    if chain:
        in_specs.append(pl.BlockSpec((1, 1, 2, BW_B), lambda i, h: (i, h, 0, 0)))
        args.append(h0)
    out_shape = [jax.ShapeDtypeStruct((n, D_MODEL), BF16)]
    out_specs = [pl.BlockSpec((ROW_BLOCK, BW_B), lambda i, h: (i, h))]
    if emit_final:
        out_shape.append(jax.ShapeDtypeStruct((nblk, H_B, 2, n_seg, BW_B), F32))
        out_specs.append(pl.BlockSpec((1, 1, 2, n_seg, BW_B), lambda i, h: (i, h, 0, 0, 0)))
    seg_rows = n_seg * SEG_PITCH
    kern = functools.partial(_lru_kernel, chain=chain, emit_final=emit_final)
    return pl.pallas_call(
        kern, out_shape=tuple(out_shape), grid=(nblk, H_B),
        in_specs=in_specs, out_specs=tuple(out_specs),
        scratch_shapes=[pltpu.VMEM((2, seg_rows, BW_B), F32)] * 4,
        compiler_params=_cparams(2),
        name="lru",
    )(*args)


def _ev_out_kernel(x_ref, yf_ref, yb_ref, xs_ref, z_ref, yl_ref, wo_ref, dsk_ref, gn_ref, mod_ref, o_ref):
    y = yf_ref[...] + yb_ref[...] + dsk_ref[...] * xs_ref[...].astype(F32)
    y = y * _silu(z_ref[...].astype(F32))
    gw = D_MODEL // G_A
    parts = []
    for g in range(G_A):
        yg = y[:, g * gw:(g + 1) * gw]
        ms = jnp.mean(yg * yg, axis=-1, keepdims=True)
        parts.append((yg * lax.rsqrt(ms + EPS) * gn_ref[:, g * gw:(g + 1) * gw]).astype(BF16))
    mix = _dot(parts[0], wo_ref[0:gw, :]) + _dot(parts[1], wo_ref[gw:2 * gw, :])
    mix = mix + _dot(yl_ref[...], wo_ref[D_MODEL:2 * D_MODEL, :])
    o_ref[...] = x_ref[...] + mod_ref[0, 2:3, :] * mix


def _ev_out(x2d, yf, yb, p, ylru, wo, dsk, gn, mod_l, cond_of_tile):
    n = x2d.shape[0]
    row = lambda i: (i, 0)
    return pl.pallas_call(
        _ev_out_kernel,
        out_shape=jax.ShapeDtypeStruct((n, D_MODEL), F32),
        grid=(n // TM,),
        in_specs=[pl.BlockSpec((TM, D_MODEL), row), pl.BlockSpec((TM, D_MODEL), row),
                  pl.BlockSpec((TM, D_MODEL), row),
                  pl.BlockSpec((TM, D_MODEL), lambda i: (i, 3)),
                  pl.BlockSpec((TM, D_MODEL), lambda i: (i, 0)),
                  pl.BlockSpec((TM, D_MODEL), row),
                  _const_spec((2 * D_MODEL, D_MODEL)), _const_spec((1, D_MODEL)), _const_spec((1, D_MODEL)),
                  pl.BlockSpec((1, 6, D_MODEL), lambda i: (cond_of_tile(i), 0, 0))],
        out_specs=pl.BlockSpec((TM, D_MODEL), row),
        compiler_params=_cparams(1),
        name="ev_out",
    )(x2d, yf, yb, p, p, ylru, wo, dsk, gn, mod_l)


def _ffn_kernel(x_ref, mod_ref, g_ref, wgu_ref, wd_ref, o_ref):
    x = x_ref[...]
    h = _modulated_norm(x, g_ref[...], mod_ref[0, 4:5, :], mod_ref[0, 3:4, :]).astype(BF16)
    n_chunk = 2
    fc = D_FF // n_chunk
    f = None
    for c in range(n_chunk):
        g = _dot(h, wgu_ref[:, c * fc:(c + 1) * fc])
        u = _dot(h, wgu_ref[:, D_FF + c * fc:D_FF + (c + 1) * fc])
        part = _dot((_silu(g) * u).astype(BF16), wd_ref[c * fc:(c + 1) * fc, :])
        f = part if f is None else f + part
    o_ref[...] = x + mod_ref[0, 5:6, :] * f


def _ffn(x2d, mod_l, cond_of_tile, gain, wgu, wd):
    n = x2d.shape[0]
    return pl.pallas_call(
        _ffn_kernel,
        out_shape=jax.ShapeDtypeStruct((n, D_MODEL), F32),
        grid=(n // TM,),
        in_specs=[pl.BlockSpec((TM, D_MODEL), lambda i: (i, 0)),
                  pl.BlockSpec((1, 6, D_MODEL), lambda i: (cond_of_tile(i), 0, 0)),
                  _const_spec((1, D_MODEL)), _const_spec((D_MODEL, 2 * D_FF)), _const_spec((D_FF, D_MODEL))],
        out_specs=pl.BlockSpec((TM, D_MODEL), lambda i: (i, 0)),
        compiler_params=_cparams(1),
        name="ffn",
    )(x2d, mod_l, gain, wgu, wd)


OD_IN_COLS = 896


def _od_inproj_kernel(*refs, use_rope):
    (x_ref, mod_ref, g_ref, wi_ref, qn_ref, kvn_ref, wq_ref), rest = refs[:7], refs[7:]
    if use_rope:
        (wqs_ref, cq_ref, sq_ref, ck_ref, sk_ref), rest = rest[:5], rest[5:]
    q_ref, ckv_ref, kr_ref = rest
    h = _modulated_norm(x_ref[...], g_ref[...], mod_ref[0, 1:2, :], mod_ref[0, 0:1, :]).astype(BF16)
    proj = _dot(h, wi_ref[...])
    cq = proj[:, 0:Q_RANK]
    cqn = (cq * lax.rsqrt(jnp.mean(cq * cq, axis=-1, keepdims=True) + EPS) * qn_ref[...]).astype(BF16)
    ckv = proj[:, Q_RANK:Q_RANK + KV_RANK]
    ckv_ref[...] = ckv * lax.rsqrt(jnp.mean(ckv * ckv, axis=-1, keepdims=True) + EPS) * kvn_ref[...]
    kr = proj[:, 640:640 + ROPE_C]
    q = _dot(cqn, wq_ref[...])
    if use_rope:
        q = q * jnp.tile(cq_ref[...], (1, H_C)) + _dot(cqn, wqs_ref[...]) * jnp.tile(sq_ref[...], (1, H_C))
        kr = kr * ck_ref[...] + proj[:, 768:768 + ROPE_C] * sk_ref[...]
    q = (q * (NOPE_C + ROPE_C) ** -0.5).astype(BF16)
    for h in range(H_C):
        q_ref[h] = q[:, h * HEAD_PAD:(h + 1) * HEAD_PAD]
    kr_ref[...] = kr


def _od_inproj(x2d, mod_l, cond_of_tile, gain, wi, qn, kvn, wq, rope, seq_len):
    n = x2d.shape[0]
    use_rope = rope is not None
    in_specs = [pl.BlockSpec((TM, D_MODEL), lambda i: (i, 0)),
                pl.BlockSpec((1, 6, D_MODEL), lambda i: (cond_of_tile(i), 0, 0)),
                _const_spec((1, D_MODEL)), _const_spec((D_MODEL, OD_IN_COLS)),
                _const_spec((1, Q_RANK)), _const_spec((1, KV_RANK)),
                _const_spec((Q_RANK, H_C * HEAD_PAD))]
    args = [x2d, mod_l, gain, wi, qn, kvn, wq]
    if use_rope:
        wqs, cos_q, sin_q, cos_k, sin_k = rope
        per_seq = seq_len // TM
        pos = lambda i: (i % per_seq, 0)
        in_specs += [_const_spec((Q_RANK, H_C * HEAD_PAD)),
                     pl.BlockSpec((TM, HEAD_PAD), pos), pl.BlockSpec((TM, HEAD_PAD), pos),
                     pl.BlockSpec((TM, ROPE_C), pos), pl.BlockSpec((TM, ROPE_C), pos)]
        args += [wqs, cos_q, sin_q, cos_k, sin_k]
    return pl.pallas_call(
        functools.partial(_od_inproj_kernel, use_rope=use_rope),
        out_shape=(jax.ShapeDtypeStruct((H_C, n, HEAD_PAD), BF16),
                   jax.ShapeDtypeStruct((n, KV_RANK), F32),
                   jax.ShapeDtypeStruct((n, ROPE_C), F32)),
        grid=(n // TM,),
        in_specs=in_specs,
        out_specs=(pl.BlockSpec((H_C, TM, HEAD_PAD), lambda i: (0, i, 0)),
                   pl.BlockSpec((TM, KV_RANK), lambda i: (i, 0)),
                   pl.BlockSpec((TM, ROPE_C), lambda i: (i, 0))),
        compiler_params=_cparams(1),
        name="od_inproj",
    )(*args)


def _kv_up_kernel(ckv_ref, kr_ref, wk_ref, ek_ref, wv_ref, k_ref, v_ref):
    ckv = ckv_ref[...].astype(BF16)
    k = (_dot(ckv, wk_ref[...]) + _dot(kr_ref[...].astype(BF16), ek_ref[...])).astype(BF16)
    v = _dot(ckv, wv_ref[...]).astype(BF16)
    for h in range(H_C):
        k_ref[h] = k[:, h * HEAD_PAD:(h + 1) * HEAD_PAD]
    for hp in range(H_C // 2):
        v_ref[hp] = v[:, hp * LANES:(hp + 1) * LANES]


def _kv_up(ckv_all, kr_all, wk, ek, wv):
    m = ckv_all.shape[0]
    return pl.pallas_call(
        _kv_up_kernel,
        out_shape=(jax.ShapeDtypeStruct((H_C, m, HEAD_PAD), BF16),
                   jax.ShapeDtypeStruct((H_C // 2, m, LANES), BF16)),
        grid=(m // TM,),
        in_specs=[pl.BlockSpec((TM, KV_RANK), lambda i: (i, 0)), pl.BlockSpec((TM, ROPE_C), lambda i: (i, 0)),
                  _const_spec((KV_RANK, H_C * HEAD_PAD)), _const_spec((ROPE_C, H_C * HEAD_PAD)),
                  _const_spec((KV_RANK, H_C * V_C))],
        out_specs=(pl.BlockSpec((H_C, TM, HEAD_PAD), lambda i: (0, i, 0)),
                   pl.BlockSpec((H_C // 2, TM, LANES), lambda i: (0, i, 0))),
        compiler_params=_cparams(1),
        name="kv_up",
    )(ckv_all, kr_all, wk, ek, wv)


def _attn_kernel(q_ref, k_ref, v_ref, o_ref):
    lane = lax.broadcasted_iota(jnp.int32, (q_ref.shape[1], LANES), 1)

    def head_pair(hp, carry):
        outs = []
        for side in range(2):
            h = 2 * hp + side
            s = lax.dot_general(q_ref[h], k_ref[h], (((1,), (1,)), ((), ())), preferred_element_type=F32)
            p = jnp.exp(s - jnp.max(s, axis=-1, keepdims=True))
            l = jnp.sum(p, axis=-1, keepdims=True)
            outs.append(_dot(p.astype(BF16), v_ref[hp]) / l)
        o_ref[hp] = jnp.where(lane < V_C, outs[0], outs[1]).astype(BF16)
        return carry

    lax.fori_loop(0, H_C // 2, head_pair, 0)


def _attention(q, k, v, n_batch, seq_len, kv_len):
    tq = 256
    per = seq_len // tq
    return pl.pallas_call(
        _attn_kernel,
        out_shape=jax.ShapeDtypeStruct((H_C // 2, q.shape[1], LANES), BF16),
        grid=(n_batch, per),
        in_specs=[pl.BlockSpec((H_C, tq, HEAD_PAD), lambda b, i: (0, b * per + i, 0)),
                  pl.BlockSpec((H_C, kv_len, HEAD_PAD), lambda b, i: (0, b, 0)),
                  pl.BlockSpec((H_C // 2, kv_len, LANES), lambda b, i: (0, b, 0))],
        out_specs=pl.BlockSpec((H_C // 2, tq, LANES), lambda b, i: (0, b * per + i, 0)),
        compiler_params=_cparams(2),
        name="attention",
    )(q, k, v)


TOKEN_TILE = (8, LANES)


def _store_token_tiles(ref, value):
    for s in range(TOKEN_TILE[0]):
        ref[:, s, :] = value[:, s * LANES:(s + 1) * LANES]


def _load_token_tiles(ref):
    return jnp.concatenate([ref[:, s, :] for s in range(TOKEN_TILE[0])], axis=-1)


ROUTE_IDX0, ROUTE_IDX1, ROUTE_RANK0, ROUTE_RANK1, ROUTE_W0, ROUTE_W1 = range(6)


def _od_out_router_kernel(o_ref, x_ref, wo_ref, mod_ref, g_ref, wr_ref, x3_ref, h2_ref, route_ref, cnt_ref, cnt_scr):
    i = pl.program_id(0)

    @pl.when(i == 0)
    def _():
        cnt_scr[...] = jnp.zeros(cnt_scr.shape, F32)

    o = jnp.concatenate([o_ref[hp] for hp in range(H_C // 2)], axis=-1)
    x3 = x_ref[...] + mod_ref[0, 2:3, :] * _dot(o, wo_ref[...])
    x3_ref[...] = x3
    h2 = _modulated_norm(x3, g_ref[...], mod_ref[0, 4:5, :], mod_ref[0, 3:4, :])
    _store_token_tiles(h2_ref, h2)
    logits = jnp.dot(h2, wr_ref[...], precision=HIGHEST, preferred_element_type=F32)
    tm = logits.shape[0]
    lane = lax.broadcasted_iota(jnp.int32, logits.shape, 1)
    neg = jnp.float32(-jnp.inf)
    logits = jnp.where(lane < N_EXPERTS, logits, neg)
    m0 = jnp.max(logits, axis=-1, keepdims=True)
    lane_f = lane.astype(F32)
    i0 = jnp.min(jnp.where(logits == m0, lane_f, float(LANES)), axis=-1, keepdims=True)
    rest = jnp.where(lane_f == i0, neg, logits)
    m1 = jnp.max(rest, axis=-1, keepdims=True)
    i1 = jnp.min(jnp.where(rest == m1, lane_f, float(LANES)), axis=-1, keepdims=True)
    e = jnp.exp(m1 - m0)
    w0 = 1.0 / (1.0 + e)
    w1 = e * w0
    oh0 = jnp.where(lane_f == i0, 1.0, 0.0).astype(F32)
    oh1 = jnp.where(lane_f == i1, 1.0, 0.0).astype(F32)
    sel = oh0 + oh1
    ri = lax.broadcasted_iota(jnp.int32, (tm, tm), 0)
    ci = lax.broadcasted_iota(jnp.int32, (tm, tm), 1)
    before = jnp.where(ci < ri, 1.0, 0.0).astype(BF16)
    prior = _dot(before, sel.astype(BF16)) + cnt_scr[0:1, :]
    r0 = jnp.sum(oh0 * prior, axis=-1, keepdims=True)
    r1 = jnp.sum(oh1 * prior, axis=-1, keepdims=True)
    route = jnp.where(lane == ROUTE_IDX0, i0, 0.0)
    route = jnp.where(lane == ROUTE_IDX1, i1, route)
    route = jnp.where(lane == ROUTE_RANK0, r0, route)
    route = jnp.where(lane == ROUTE_RANK1, r1, route)
    route = jnp.where(lane == ROUTE_W0, w0, route)
    route = jnp.where(lane == ROUTE_W1, w1, route)
    route_ref[...] = route
    total = cnt_scr[0:1, :] + jnp.sum(sel, axis=0, keepdims=True)
    cnt_scr[...] = jnp.broadcast_to(total, cnt_scr.shape)
    cnt_ref[...] = cnt_scr[...]


def _od_out_router(o, x2d, wo, mod_l, cond_of_tile, gain, wr):
    n = x2d.shape[0]
    row = lambda i: (i, 0)
    return pl.pallas_call(
        _od_out_router_kernel,
        out_shape=(jax.ShapeDtypeStruct((n, D_MODEL), F32), jax.ShapeDtypeStruct((n,) + TOKEN_TILE, F32),
                   jax.ShapeDtypeStruct((n, LANES), F32), jax.ShapeDtypeStruct((8, LANES), F32)),
        grid=(n // TM,),
        in_specs=[pl.BlockSpec((H_C // 2, TM, LANES), lambda i: (0, i, 0)), pl.BlockSpec((TM, D_MODEL), row),
                  _const_spec((D_MODEL, D_MODEL)),
                  pl.BlockSpec((1, 6, D_MODEL), lambda i: (cond_of_tile(i), 0, 0)),
                  _const_spec((1, D_MODEL)), _const_spec((D_MODEL, LANES))],
        out_specs=(pl.BlockSpec((TM, D_MODEL), row), pl.BlockSpec((TM,) + TOKEN_TILE, lambda i: (i, 0, 0)),
                   pl.BlockSpec((TM, LANES), row), pl.BlockSpec((8, LANES), lambda i: (0, 0))),
        scratch_shapes=[pltpu.VMEM((8, LANES), F32)],
        compiler_params=_cparams(1),
        name="od_out_router",
    )(o, x2d, wo, mod_l, gain, wr)


def _row_copy(src, s, dst, d, sem):
    return pltpu.make_async_copy(src.at[pl.ds(s, 1)], dst.at[pl.ds(d, 1)], sem)


def _move_rows(src_ref, dst_ref, sem, src_row, dst_row):
    n_batches = ROUTE_BLOCK // ROUTE_BATCH

    def drain():
        for _ in range(2 * ROUTE_BATCH):
            _row_copy(src_ref, 0, dst_ref, 0, sem).wait()

    def batch(b, carry):
        for j in range(ROUTE_BATCH):
            t = b * ROUTE_BATCH + j
            for k in range(2):
                _row_copy(src_ref, src_row(t, k), dst_ref, dst_row(t, k), sem).start()

        @pl.when(b > 0)
        def _():
            drain()
        return carry

    lax.fori_loop(0, n_batches, batch, 0)
    drain()


def _scatter_rows_kernel(pos_ref, src_ref, dst_in_ref, dst_ref, sem):
    del dst_in_ref
    base = pl.program_id(0) * ROUTE_BLOCK
    _move_rows(src_ref, dst_ref, sem, lambda t, k: base + t, lambda t, k: pos_ref[2 * t + k])


def _scatter_rows(pos_flat, src, dst):
    n = src.shape[0]
    return pl.pallas_call(
        _scatter_rows_kernel,
        out_shape=jax.ShapeDtypeStruct(dst.shape, dst.dtype),
        grid=(n // ROUTE_BLOCK,),
        in_specs=[pl.BlockSpec((2 * ROUTE_BLOCK,), lambda i: (i,), memory_space=pltpu.SMEM),
                  pl.BlockSpec(memory_space=pl.ANY), pl.BlockSpec(memory_space=pl.ANY)],
        out_specs=pl.BlockSpec(memory_space=pl.ANY),
        scratch_shapes=[pltpu.SemaphoreType.DMA(())],
        input_output_aliases={2: 0},
        compiler_params=pltpu.CompilerParams(dimension_semantics=("arbitrary",), has_side_effects=True),
        name="scatter_rows",
    )(pos_flat, src, dst)


def _gather_rows_kernel(pos_ref, src_ref, dst_ref, sem, *, n_tokens):
    base = pl.program_id(0) * ROUTE_BLOCK
    _move_rows(src_ref, dst_ref, sem, lambda t, k: pos_ref[2 * t + k], lambda t, k: k * n_tokens + base + t)


def _gather_rows(pos_flat, src, n_tokens):
    return pl.pallas_call(
        functools.partial(_gather_rows_kernel, n_tokens=n_tokens),
        out_shape=jax.ShapeDtypeStruct((2 * n_tokens,) + src.shape[1:], src.dtype),
        grid=(n_tokens // ROUTE_BLOCK,),
        in_specs=[pl.BlockSpec((2 * ROUTE_BLOCK,), lambda i: (i,), memory_space=pltpu.SMEM),
                  pl.BlockSpec(memory_space=pl.ANY)],
        out_specs=pl.BlockSpec(memory_space=pl.ANY),
        scratch_shapes=[pltpu.SemaphoreType.DMA(())],
        compiler_params=pltpu.CompilerParams(dimension_semantics=("arbitrary",), has_side_effects=True),
        name="gather_rows",
    )(pos_flat, src)


def _experts_kernel(te_ref, nt_ref, h_ref, wg_ref, wu_ref, wd_ref, y_ref, h_scr, acc_scr):
    i, c = pl.program_id(0), pl.program_id(1)

    @pl.when(i < nt_ref[0])
    def _():
        @pl.when(c == 0)
        def _():
            h_scr[...] = _load_token_tiles(h_ref).astype(BF16)

        h = h_scr[...]
        g = _dot(h, wg_ref[0])
        u = _dot(h, wu_ref[0])
        part = _dot((_silu(g) * u).astype(BF16), wd_ref[0])

        @pl.when(c == 0)
        def _():
            acc_scr[...] = part

        @pl.when(c > 0)
        def _():
            acc_scr[...] += part

        @pl.when(c == pl.num_programs(1) - 1)
        def _():
            _store_token_tiles(y_ref, acc_scr[...])

    @pl.when((i >= nt_ref[0]) & (c == pl.num_programs(1) - 1))
    def _():
        y_ref[...] = jnp.zeros(y_ref.shape, F32)


def _experts(tile_expert, n_tiles, hs, wgu, wd):
    rows = hs.shape[0]
    n_c = D_FF_E // FF_CHUNK_E
    live = lambda i, nt: jnp.minimum(i, nt[0] - 1)
    grid_spec = pltpu.PrefetchScalarGridSpec(
        num_scalar_prefetch=2,
        grid=(rows // TM_E, n_c),
        in_specs=[pl.BlockSpec((TM_E,) + TOKEN_TILE, lambda i, c, te, nt: (live(i, nt), 0, 0)),
                  pl.BlockSpec((1, D_MODEL, FF_CHUNK_E), lambda i, c, te, nt: (te[live(i, nt)], 0, c)),
                  pl.BlockSpec((1, D_MODEL, FF_CHUNK_E), lambda i, c, te, nt: (te[live(i, nt)], 0, n_c + c)),
                  pl.BlockSpec((1, FF_CHUNK_E, D_MODEL), lambda i, c, te, nt: (te[live(i, nt)], c, 0))],
        out_specs=pl.BlockSpec((TM_E,) + TOKEN_TILE, lambda i, c, te, nt: (i, 0, 0)),
        scratch_shapes=[pltpu.VMEM((TM_E, D_MODEL), BF16), pltpu.VMEM((TM_E, D_MODEL), F32)],
    )
    return pl.pallas_call(
        _experts_kernel,
        out_shape=jax.ShapeDtypeStruct((rows,) + TOKEN_TILE, F32),
        grid_spec=grid_spec,
        compiler_params=_cparams(2),
        name="experts",
    )(tile_expert, n_tiles, hs, wgu, wgu, wd)


def _combine_kernel(x_ref, ya_ref, yb_ref, route_ref, mod_ref, g_ref, o_ref):
    w0 = route_ref[:, ROUTE_W0:ROUTE_W0 + 1]
    w1 = route_ref[:, ROUTE_W1:ROUTE_W1 + 1]
    x = x_ref[...] + mod_ref[0, 5:6, :] * (w0 * _load_token_tiles(ya_ref) + w1 * _load_token_tiles(yb_ref))
    o_ref[...] = x * lax.rsqrt(jnp.mean(x * x, axis=-1, keepdims=True) + EPS) * g_ref[...]


def _combine(x3, y2, route, mod_l, cond_of_tile, final_g):
    n = x3.shape[0]
    nt = n // TM
    return pl.pallas_call(
        _combine_kernel,
        out_shape=jax.ShapeDtypeStruct((n, D_MODEL), F32),
        grid=(nt,),
        in_specs=[pl.BlockSpec((TM, D_MODEL), lambda i: (i, 0)),
                  pl.BlockSpec((TM,) + TOKEN_TILE, lambda i: (i, 0, 0)),
                  pl.BlockSpec((TM,) + TOKEN_TILE, lambda i: (nt + i, 0, 0)),
                  pl.BlockSpec((TM, LANES), lambda i: (i, 0)),
                  pl.BlockSpec((1, 6, D_MODEL), lambda i: (cond_of_tile(i), 0, 0)),
                  _const_spec((1, D_MODEL))],
        out_specs=pl.BlockSpec((TM, D_MODEL), lambda i: (i, 0)),
        compiler_params=_cparams(1),
        name="combine",
    )(x3, y2, y2, route, mod_l, final_g)


def _even_params(ev_w_in, ev_conv_a_w, ev_conv_a_b, ev_a_log, ev_dt_bias, ev_d_skip, ev_gnorm,
                 ev_conv_b_w, ev_conv_b_b, ev_lru_w, ev_lru_b, ev_lru_lam, ev_w_out, ev_ffn_gu, ev_ffn_down):
    d_a = H_A * P_A
    xbc = d_a + 2 * G_A * N_A
    o_xbc, o_dt, o_gate, o_xb = d_a, d_a + xbc, d_a + xbc + 2 * H_A, d_a + xbc + 2 * H_A + D_MODEL
    w = ev_w_in
    w_main = jnp.concatenate([w[:, :d_a], w[:, o_xb:], w[:, o_gate:o_xb], w[:, o_xbc:o_dt]], axis=1).astype(BF16)
    w_dt = jnp.pad(w[:, o_dt:o_gate], ((0, 0), (0, LANES - 2 * H_A))).astype(BF16)
    zeros_w = jnp.zeros((CONV_K, D_MODEL), F32)
    cw = jnp.concatenate([zeros_w, ev_conv_b_w, zeros_w, ev_conv_a_w], axis=1)
    zeros_b = jnp.zeros((D_MODEL,), F32)
    cb = jnp.concatenate([zeros_b, ev_conv_b_b, zeros_b, ev_conv_a_b])[None, :]
    dtb = jnp.pad(ev_dt_bias.reshape(-1), (0, LANES - 2 * H_A))[None, :]
    alog = jnp.pad(ev_a_log.reshape(-1), (0, LANES - 2 * H_A))[None, :]
    j = jnp.arange(2 * LANES) % LANES
    c = jnp.arange(d_a) // P_A
    e2 = jnp.stack([(j[:, None] == (d * H_A + c)[None, :]) for d in range(2)]).astype(BF16)
    wl = jnp.transpose(ev_lru_w, (2, 3, 0, 1, 4)).reshape(H_B, BW_B, 4 * BW_B).astype(BF16)
    lb = jnp.transpose(ev_lru_b.reshape(2, 2, H_B, BW_B), (2, 0, 1, 3)).reshape(H_B, 1, 4 * BW_B)
    lam = jnp.transpose(ev_lru_lam.reshape(2, H_B, BW_B), (1, 0, 2))
    dsk = jnp.repeat(ev_d_skip, P_A)[None, :]
    return dict(w_main=w_main, w_dt=w_dt, cw=cw, cb=cb, dtb=dtb, alog=alog, e2=e2, wl=wl, lb=lb, lam=lam,
                dsk=dsk, gn=ev_gnorm[None, :], w_out=ev_w_out.astype(BF16),
                wgu=ev_ffn_gu.astype(BF16), wd=ev_ffn_down.astype(BF16))


def _rope_partner(w, lo):
    blk = w[:, lo:lo + ROPE_C].reshape(w.shape[0], 2, 2, N_FREQ)
    return jnp.flip(blk, axis=2).reshape(w.shape[0], ROPE_C)


def _odd_params(od_w_in, od_q_norm, od_w_q_up, od_kv_norm, od_w_kv_up, od_w_o, od_router):
    kr_lo = Q_RANK + KV_RANK
    wi = jnp.zeros((D_MODEL, OD_IN_COLS), F32)
    wi = wi.at[:, :kr_lo + ROPE_C].set(od_w_in)
    wi = wi.at[:, 768:768 + ROPE_C].set(_rope_partner(od_w_in, kr_lo))
    wq3 = od_w_q_up.reshape(Q_RANK, H_C, NOPE_C + ROPE_C)
    pad = HEAD_PAD - NOPE_C - ROPE_C
    wq = jnp.pad(wq3, ((0, 0), (0, 0), (0, pad))).reshape(Q_RANK, H_C * HEAD_PAD)
    partner = jnp.flip(wq3[:, :, NOPE_C:].reshape(Q_RANK, H_C, 2, 2, N_FREQ), axis=3).reshape(Q_RANK, H_C, ROPE_C)
    wqs = jnp.pad(partner, ((0, 0), (0, 0), (NOPE_C, pad))).reshape(Q_RANK, H_C * HEAD_PAD)
    wkv3 = od_w_kv_up.reshape(KV_RANK, H_C, NOPE_C + V_C)
    wk = jnp.pad(wkv3[:, :, :NOPE_C], ((0, 0), (0, 0), (0, HEAD_PAD - NOPE_C))).reshape(KV_RANK, H_C * HEAD_PAD)
    wv = wkv3[:, :, NOPE_C:].reshape(KV_RANK, H_C * V_C)
    ek_head = jnp.pad(jnp.eye(ROPE_C, dtype=F32), ((0, 0), (NOPE_C, pad)))
    ek = jnp.tile(ek_head, (1, H_C))
    wr = jnp.pad(od_router, ((0, 0), (0, LANES - N_EXPERTS)))
    return dict(wi=wi.astype(BF16), qn=od_q_norm[None, :], kvn=od_kv_norm[None, :], wq=wq.astype(BF16),
                wqs=wqs.astype(BF16), wk=wk.astype(BF16), ek=ek.astype(BF16), wv=wv.astype(BF16),
                wo=od_w_o.astype(BF16), wr=wr)


def _rope_tables(n_tokens):
    rows = n_tokens // GRID_W
    row = jnp.repeat(jnp.arange(rows), GRID_W).astype(F32)
    col = jnp.tile(jnp.arange(GRID_W), rows).astype(F32)
    inv = ROPE_BASE ** (-jnp.arange(N_FREQ, dtype=F32) / N_FREQ)
    ang_r, ang_c = row[:, None] * inv, col[:, None] * inv
    cos_k = jnp.concatenate([jnp.cos(ang_r)] * 2 + [jnp.cos(ang_c)] * 2, axis=1)
    sin_k = jnp.concatenate([-jnp.sin(ang_r), jnp.sin(ang_r), -jnp.sin(ang_c), jnp.sin(ang_c)], axis=1)
    pad = HEAD_PAD - NOPE_C - ROPE_C
    cos_q = jnp.pad(cos_k, ((0, 0), (NOPE_C, pad)), constant_values=1.0)
    sin_q = jnp.pad(sin_k, ((0, 0), (NOPE_C, pad)))
    return cos_q, sin_q, cos_k, sin_k


def _layer0(x2d, mod_l, cond_row, norm_g0, ev, n_batch, seq_len, h0_ssd, h0_lru, is_ctx):
    per_block = max(ROW_BLOCK // seq_len, 1)
    blocks_per_seq = max(seq_len // ROW_BLOCK, 1)
    cond_of_block = (lambda i: 0) if is_ctx else (lambda i: cond_row(i // blocks_per_seq))
    tiles_per_seq = seq_len // TM if seq_len >= TM else 1
    cond_of_tile = (lambda i: 0) if is_ctx else (lambda i: cond_row(i // tiles_per_seq))
    p, dt = _ev_inproj(x2d, mod_l, cond_of_block, norm_g0[0:1], ev["w_main"], ev["w_dt"], ev["cw"], ev["cb"],
                       ev["dtb"], seq_len)
    ssd_out = _ssd(p, dt, ev["alog"], ev["e2"], h0_ssd, n_batch, seq_len, emit_final=is_ctx)
    lru_out = _lru(p, ev["wl"], ev["lb"], ev["lam"], h0_lru, chain=not is_ctx, emit_final=is_ctx)
    x1 = _ev_out(x2d, ssd_out[0], ssd_out[1], p, lru_out[0], ev["w_out"], ev["dsk"], ev["gn"], mod_l, cond_of_tile)
    x2 = _ffn(x1, mod_l, cond_of_tile, norm_g0[1:2], ev["wgu"], ev["wd"])
    finals = None
    if is_ctx:
        hp = H_A * P_A
        s_ssd = ssd_out[2].reshape(n_batch, 1, 2, H_A, P_A, N_A)
        fin = lru_out[1]
        s_lru = jnp.transpose(fin, (0, 3, 2, 1, 4)).reshape(n_batch, 1, 2, H_B * BW_B)
        finals = (s_ssd, s_lru)
        del hp, per_block
    return x2, finals, cond_of_tile


def _layer1_pre(x2d, mod_l, cond_of_tile, norm_g1, od, n_batch, seq_len, rope, cache):
    q, ckv, kr = _od_inproj(x2d, mod_l, cond_of_tile, norm_g1[0:1], od["wi"], od["qn"], od["kvn"], od["wq"],
                            rope, seq_len)
    if cache is None:
        ckv_all, kr_all, kv_len = ckv, kr, seq_len
    else:
        c_ckv, c_kr = cache
        past = c_ckv.shape[1]
        kv_len = past + seq_len
        ckv_all = jnp.concatenate([c_ckv, ckv.reshape(n_batch, seq_len, KV_RANK)], axis=1).reshape(-1, KV_RANK)
        kr_all = jnp.concatenate([c_kr, kr.reshape(n_batch, seq_len, ROPE_C)], axis=1).reshape(-1, ROPE_C)
    k, v = _kv_up(ckv_all, kr_all, od["wk"], od["ek"], od["wv"])
    o = _attention(q, k, v, n_batch, seq_len, kv_len)
    x3, h2, route, counts = _od_out_router(o, x2d, od["wo"], mod_l, cond_of_tile, norm_g1[1:2], od["wr"])
    return x3, h2, route, counts, ckv, kr


def kernel(x_prompt, x_sample, state_ssd, state_lru, cache_ckv, cache_krope, c, c_ctx, mod_w, mod_b, norm_g, final_g, ev_w_in, ev_conv_a_w, ev_conv_a_b, ev_a_log, ev_dt_bias, ev_d_skip, ev_gnorm, ev_conv_b_w, ev_conv_b_b, ev_lru_w, ev_lru_b, ev_lru_lam, ev_w_out, ev_ffn_gu, ev_ffn_down, od_w_in, od_q_norm, od_w_q_up, od_kv_norm, od_w_kv_up, od_w_o, od_router, od_moe_gu, od_moe_down):
    batch, seq, _ = x_prompt.shape
    dbatch, dseq, _ = x_sample.shape
    n_ctx, n_dec = batch * seq, dbatch * dseq

    cond = jnp.concatenate([c_ctx[None, :], c, jnp.zeros((16 - 1 - dbatch, D_MODEL), F32)], axis=0)
    mod = _modulation(cond, mod_w, mod_b)
    ev = _even_params(ev_w_in[0], ev_conv_a_w[0], ev_conv_a_b[0], ev_a_log[0], ev_dt_bias[0], ev_d_skip[0],
                      ev_gnorm[0], ev_conv_b_w[0], ev_conv_b_b[0], ev_lru_w[0], ev_lru_b[0], ev_lru_lam[0],
                      ev_w_out[0], ev_ffn_gu[0], ev_ffn_down[0])
    od = _odd_params(od_w_in[0], od_q_norm[0], od_w_q_up[0], od_kv_norm[0], od_w_kv_up[0], od_w_o[0], od_router[0])
    wgu_e = od_moe_gu[0].astype(BF16)
    wd_e = od_moe_down[0].astype(BF16)
    dec_row = lambda b: 1 + b

    xc, finals, tile_c = _layer0(x_prompt.reshape(n_ctx, D_MODEL), mod[0], None, norm_g[0], ev, batch, seq,
                                 None, None, True)
    h0_ssd = state_ssd[:, 0].reshape(dbatch, 2, H_A * P_A, N_A)
    h0_lru = jnp.transpose(state_lru[:, 0].reshape(dbatch, 2, H_B, BW_B), (0, 2, 1, 3))
    xd, _, tile_d = _layer0(x_sample.reshape(n_dec, D_MODEL), mod[0], dec_row, norm_g[0], ev, dbatch, dseq,
                            h0_ssd, h0_lru, False)
    new_state_ssd, new_state_lru = finals

    rope = (od["wqs"],) + _rope_tables(dseq)
    x3c, h2c, route_c, cnt_c, ckv_c, kr_c = _layer1_pre(xc, mod[1], tile_c, norm_g[1], od, batch, seq, None, None)
    x3d, h2d, route_d, cnt_d, _, _ = _layer1_pre(xd, mod[1], tile_d, norm_g[1], od, dbatch, dseq, rope,
                                                  (cache_ckv[:, 0], cache_krope[:, 0]))
    new_cache_ckv = ckv_c.reshape(batch, 1, seq, KV_RANK)
    new_cache_krope = kr_c.reshape(batch, 1, seq, ROPE_C)

    n_pairs = 2 * (n_ctx + n_dec)
    n_tiles_max = n_pairs // TM_E + N_EXPERTS
    cnt_c8 = cnt_c[0, :N_EXPERTS].astype(jnp.int32)
    cnt_d8 = cnt_d[0, :N_EXPERTS].astype(jnp.int32)
    tiles_e = (cnt_c8 + cnt_d8 + TM_E - 1) // TM_E
    tile_end = jnp.cumsum(tiles_e)
    row_start = (tile_end - tiles_e) * TM_E
    n_tiles = tile_end[-1:].astype(jnp.int32)
    tile_expert = jnp.minimum(jnp.searchsorted(tile_end, jnp.arange(n_tiles_max), side="right"),
                              N_EXPERTS - 1).astype(jnp.int32)

    def positions(route, extra):
        idx = route[:, ROUTE_IDX0:ROUTE_IDX1 + 1].astype(jnp.int32)
        rank = route[:, ROUTE_RANK0:ROUTE_RANK1 + 1].astype(jnp.int32)
        return ((row_start + extra)[idx] + rank).reshape(-1)

    pos_c = positions(route_c, jnp.zeros_like(cnt_c8))
    pos_d = positions(route_d, cnt_c8)
    hs = jnp.zeros((n_tiles_max * TM_E,) + TOKEN_TILE, F32)
    hs = _scatter_rows(pos_c, h2c, hs)
    hs = _scatter_rows(pos_d, h2d, hs)
    ys = _experts(tile_expert, n_tiles, hs, wgu_e, wd_e)
    y2c = _gather_rows(pos_c, ys, n_ctx)
    y2d = _gather_rows(pos_d, ys, n_dec)

    y_prompt = _combine(x3c, y2c, route_c, mod[1], tile_c, final_g[None, :]).reshape(batch, seq, D_MODEL)
    y_sample = _combine(x3d, y2d, route_d, mod[1], tile_d, final_g[None, :]).reshape(dbatch, dseq, D_MODEL)
    return (y_prompt, y_sample, new_state_ssd, new_state_lru, new_cache_ckv, new_cache_krope)
```

```python
import functools
import math

import jax
import jax.numpy as jnp
from jax import lax
from jax.experimental import pallas as pl
from jax.experimental.pallas import tpu as pltpu

F32 = jnp.float32
BF16 = jnp.bfloat16
HIGHEST = lax.Precision.HIGHEST

D_MODEL = 1024
GRID_W = 64
P_A = 64
H_A = 16
G_A = 2
N_A = 128
CHUNK = 128
CONV_K = 4
H_B = 8
BW_B = 128
LRU_C = 8.0
H_C = 16
Q_RANK = 384
KV_RANK = 256
NOPE_C = 64
ROPE_C = 32
V_C = 64
N_FREQ = ROPE_C // 4
ROPE_BASE = 10000.0
D_FF = 2816
N_EXPERTS = 8
D_FF_E = 3584
EPS = 1e-6

LANES = 128
HEAD_PAD = 128
ROW_BLOCK = 2048
ROW_CHUNK = 256
SEG_PITCH = ROW_CHUNK + 8
TM = 512
TM_E = 1024
FF_CHUNK_E = 512
VMEM_LIMIT = 52 * 1024 * 1024


def _cparams(n_axes, vmem=VMEM_LIMIT):
    return pltpu.CompilerParams(dimension_semantics=("arbitrary",) * n_axes, vmem_limit_bytes=vmem)


def _dot(a, b):
    return jnp.dot(a, b, preferred_element_type=F32)


def _sigmoid(x):
    return 1.0 / (1.0 + jnp.exp(-x))


def _silu(x):
    return x * _sigmoid(x)


def _softplus(x):
    return jnp.maximum(x, 0.0) + jnp.log(1.0 + jnp.exp(-jnp.abs(x)))


def _gelu_tanh(x):
    return 0.5 * x * (1.0 + jnp.tanh(math.sqrt(2.0 / math.pi) * (x + 0.044715 * (x * x * x))))


def _modulated_norm(x, gain, scale, shift):
    ms = jnp.mean(x * x, axis=-1, keepdims=True)
    return (x * lax.rsqrt(ms + EPS)) * (gain * (1.0 + scale)) + shift


def _const_spec(shape):
    zeros = (0,) * len(shape)
    return pl.BlockSpec(shape, lambda *_: zeros, pipeline_mode=pl.Buffered(1))


def _mod_kernel(c_ref, w_ref, b_ref, o_ref):
    c = c_ref[...]
    o_ref[0] = _dot(_silu(c).astype(BF16), w_ref[0].astype(BF16)) + b_ref[0]


def _modulation(cond, mod_w, mod_b):
    depth = mod_w.shape[0]
    n = 6 * D_MODEL
    tn = n // 4
    out = pl.pallas_call(
        _mod_kernel,
        out_shape=jax.ShapeDtypeStruct((depth, 16, n), F32),
        grid=(depth, 4),
        in_specs=[pl.BlockSpec((16, D_MODEL), lambda l, j: (0, 0)),
                  pl.BlockSpec((1, D_MODEL, tn), lambda l, j: (l, 0, j)),
                  pl.BlockSpec((1, 1, tn), lambda l, j: (l, 0, j))],
        out_specs=pl.BlockSpec((1, 16, tn), lambda l, j: (l, 0, j)),
        compiler_params=_cparams(2),
        name="modulation",
    )(cond, mod_w, mod_b.reshape(depth, 1, n))
    return out.reshape(depth, 16, 6, D_MODEL)


def _ev_inproj_kernel(x_ref, mod_ref, g_ref, w_ref, wdt_ref, cw_ref, cb_ref, dtb_ref,
                      p_ref, dt_ref, h_scr, acc_scr, *, seq_is_chunk):
    j = pl.program_id(1)
    rows = h_scr.shape[0]
    n_chunks = rows // ROW_CHUNK
    width = w_ref.shape[1]

    @pl.when(j == 0)
    def _():
        gain, scale, shift = g_ref[...], mod_ref[0, 1:2, :], mod_ref[0, 0:1, :]

        def body(r, carry):
            r0 = pl.multiple_of(r * ROW_CHUNK, ROW_CHUNK)
            h = _modulated_norm(x_ref[pl.ds(r0, ROW_CHUNK), :], gain, scale, shift)
            h_scr[pl.ds(r0, ROW_CHUNK), :] = h.astype(BF16)
            return carry

        lax.fori_loop(0, n_chunks, body, 0)
        dt = _softplus(_dot(h_scr[...], wdt_ref[...]) + dtb_ref[...])
        lane = lax.broadcasted_iota(jnp.int32, dt.shape, 1)
        dt_ref[...] = jnp.where(lane < 2 * H_A, dt, 0.0)
        acc_scr[0:8, :] = jnp.zeros((8, width), F32)
        acc_scr[rows + 8:rows + 16, :] = jnp.zeros((8, width), F32)

    acc_scr[8:rows + 8, :] = _dot(h_scr[...], w_ref[...])

    def plain(act):
        def body(r, carry):
            r0 = pl.multiple_of(r * ROW_CHUNK, ROW_CHUNK)
            v = acc_scr[pl.ds(r0 + 8, ROW_CHUNK), :]
            p_ref[pl.ds(r0, ROW_CHUNK), :] = act(v).astype(BF16)
            return carry
        lax.fori_loop(0, n_chunks, body, 0)

    def conv(act):
        win = ROW_CHUNK + 16
        cw = cw_ref[...]
        cb = cb_ref[...]

        def body(r, carry):
            r0 = pl.multiple_of(r * ROW_CHUNK, ROW_CHUNK)
            xw = acc_scr[pl.ds(r0, win), :]
            if seq_is_chunk:
                row = lax.broadcasted_iota(jnp.int32, xw.shape, 0)
                xw = jnp.where((row >= 8) & (row < ROW_CHUNK + 8), xw, 0.0)
            y = cb + cw[1:2, :] * xw[8:ROW_CHUNK + 8]
            for k in (0, 2, 3):
                sh = pltpu.roll(xw, (1 - k) % win, 0)[8:ROW_CHUNK + 8]
                y = y + cw[k:k + 1, :] * sh
            p_ref[pl.ds(r0, ROW_CHUNK), :] = act(y).astype(BF16)
            return carry
        lax.fori_loop(0, n_chunks, body, 0)

    @pl.when(j < 2)
    def _():
        plain(lambda v: v)

    @pl.when((j >= 2) & (j < 4))
    def _():
        conv(lambda v: v)

    @pl.when((j >= 4) & (j < 6))
    def _():
        plain(_gelu_tanh)

    @pl.when(j >= 6)
    def _():
        conv(_silu)


P_COLS = 4608


def _ev_inproj(x2d, mod_l, cond_of_block, gain, w, wdt, cw, cb, dtb, seq_len):
    n = x2d.shape[0]
    nblk = n // ROW_BLOCK
    tn = 512
    kern = functools.partial(_ev_inproj_kernel, seq_is_chunk=(seq_len == ROW_CHUNK))
    return pl.pallas_call(
        kern,
        out_shape=(jax.ShapeDtypeStruct((n, P_COLS), BF16), jax.ShapeDtypeStruct((n, LANES), F32)),
        grid=(nblk, P_COLS // tn),
        in_specs=[pl.BlockSpec((ROW_BLOCK, D_MODEL), lambda i, j: (i, 0)),
                  pl.BlockSpec((1, 6, D_MODEL), lambda i, j: (cond_of_block(i), 0, 0)),
                  pl.BlockSpec((1, D_MODEL), lambda i, j: (0, 0)),
                  pl.BlockSpec((D_MODEL, tn), lambda i, j: (0, j)),
                  pl.BlockSpec((D_MODEL, LANES), lambda i, j: (0, 0)),
                  pl.BlockSpec((CONV_K, tn), lambda i, j: (0, j)),
                  pl.BlockSpec((1, tn), lambda i, j: (0, j)),
                  pl.BlockSpec((1, LANES), lambda i, j: (0, 0))],
        out_specs=(pl.BlockSpec((ROW_BLOCK, tn), lambda i, j: (i, j)),
                   pl.BlockSpec((ROW_BLOCK, LANES), lambda i, j: (i, 0))),
        scratch_shapes=[pltpu.VMEM((ROW_BLOCK, D_MODEL), BF16),
                        pltpu.VMEM((ROW_BLOCK + 16, tn), F32)],
        compiler_params=_cparams(2),
        name="ev_inproj",
    )(x2d, mod_l, gain, w, wdt, cw, cb, dtb)


def _ssd_direction(d, xs_ref, bm_ref, cm_ref, dt_ref, arate, e2, h_scr, y_ref):
    xs = xs_ref[...].astype(F32)
    bm = bm_ref[...]
    cm = cm_ref[...]
    a = dt_ref[...] * arate
    li = lax.broadcasted_iota(jnp.int32, (CHUNK, CHUNK), 0)
    si = lax.broadcasted_iota(jnp.int32, (CHUNK, CHUNK), 1)
    keep = (si <= li) if d == 0 else (si >= li)
    tri = jnp.where(keep, 1.0, 0.0).astype(F32)
    acs = jnp.dot(tri, a, precision=HIGHEST, preferred_element_type=F32)
    acs_t = acs.T
    tot = acs[CHUNK - 1:CHUNK, :] if d == 0 else acs[0:1, :]
    stack = jnp.concatenate([dt_ref[...], jnp.exp(acs), jnp.exp(tot - acs)], axis=0)
    hi = stack.astype(BF16)
    lo = (stack - hi.astype(F32)).astype(BF16)
    ex = _dot(jnp.concatenate([hi, lo], axis=1), e2)
    dt_e, eacs_e, ds_e = ex[0:CHUNK], ex[CHUNK:2 * CHUNK], ex[2 * CHUNK:3 * CHUNK]
    xdt = xs * dt_e
    xdt_b = xdt.astype(BF16)
    xds_b = (xdt * ds_e).astype(BF16)
    lane = lax.broadcasted_iota(jnp.int32, (CHUNK, LANES), 1)
    gw = (H_A // G_A) * P_A
    y_groups = []
    for g in range(G_A):
        bm_g = bm[:, g * N_A:(g + 1) * N_A]
        cm_g = cm[:, g * N_A:(g + 1) * N_A]
        cb = lax.dot_general(cm_g, bm_g, (((1,), (1,)), ((), ())), preferred_element_type=F32)
        h_prev = h_scr[d, :, g * gw:(g + 1) * gw]
        y_off = _dot(cm_g, h_prev.astype(BF16)) * eacs_e[:, g * gw:(g + 1) * gw]
        bm_t = bm_g.astype(F32).T.astype(BF16)
        st = _dot(bm_t, xds_b[:, g * gw:(g + 1) * gw])
        decay_tot = eacs_e[CHUNK - 1:CHUNK, g * gw:(g + 1) * gw] if d == 0 else eacs_e[0:1, g * gw:(g + 1) * gw]
        h_scr[d, :, g * gw:(g + 1) * gw] = decay_tot * h_prev + st
        pairs = []
        for q in range(H_A // G_A // 2):
            ms = []
            for e in (g * 8 + 2 * q, g * 8 + 2 * q + 1):
                col = acs[:, d * H_A + e:d * H_A + e + 1]
                row = acs_t[d * H_A + e:d * H_A + e + 1, :]
                dec = jnp.where(keep, jnp.exp(jnp.minimum(col - row, 0.0)), 0.0)
                ms.append((cb * dec).astype(BF16))
            lhs = jnp.concatenate(ms, axis=1)
            xp = xdt_b[:, (g * 8 + 2 * q) * P_A:(g * 8 + 2 * q + 2) * P_A]
            zero = jnp.zeros_like(xp)
            rhs = jnp.concatenate([jnp.where(lane < P_A, xp, zero), jnp.where(lane >= P_A, xp, zero)], axis=0)
            pairs.append(_dot(lhs, rhs))
        y_groups.append(jnp.concatenate(pairs, axis=1) + y_off)
    y_ref[...] = jnp.concatenate(y_groups, axis=1)


def _ssd_kernel(*refs, has_h0, emit_final):
    (xsf, bmf, cmf, dtf, xsb, bmb, cmb, dtb, alog_ref, e2_ref), rest = refs[:10], refs[10:]
    if has_h0:
        h0_ref, rest = rest[0], rest[1:]
    yf_ref, yb_ref, rest = rest[0], rest[1], rest[2:]
    if emit_final:
        hfin_ref, rest = rest[0], rest[1:]
    h_scr = rest[0]
    i = pl.program_id(1)

    @pl.when(i == 0)
    def _():
        for d in range(2):
            if has_h0:
                h_scr[d] = h0_ref[0, d].T
            else:
                h_scr[d] = jnp.zeros(h_scr.shape[1:], F32)

    arate = -jnp.exp(alog_ref[...])
    _ssd_direction(0, xsf, bmf, cmf, dtf, arate, e2_ref[0], h_scr, yf_ref)
    _ssd_direction(1, xsb, bmb, cmb, dtb, arate, e2_ref[1], h_scr, yb_ref)

    if emit_final:
        @pl.when(i == pl.num_programs(1) - 1)
        def _():
            for d in range(2):
                hfin_ref[0, d] = h_scr[d].T


def _ssd(p, dt, alog_row, e2, h0, n_batch, seq_len, emit_final):
    n = p.shape[0]
    nc = seq_len // CHUNK
    hp = H_A * P_A
    fwd = lambda b, i: b * nc + i
    bwd = lambda b, i: b * nc + (nc - 1 - i)
    xs_blk, bm_blk, cm_blk = 3, 16, 17

    def specs(rowf):
        return [pl.BlockSpec((CHUNK, hp), lambda b, i: (rowf(b, i), xs_blk)),
                pl.BlockSpec((CHUNK, G_A * N_A), lambda b, i: (rowf(b, i), bm_blk)),
                pl.BlockSpec((CHUNK, G_A * N_A), lambda b, i: (rowf(b, i), cm_blk)),
                pl.BlockSpec((CHUNK, LANES), lambda b, i: (rowf(b, i), 0))]

    in_specs = specs(fwd) + specs(bwd) + [pl.BlockSpec((1, LANES), lambda b, i: (0, 0)),
                                          pl.BlockSpec((2, 2 * LANES, hp), lambda b, i: (0, 0, 0))]
    args = [p, p, p, dt, p, p, p, dt, alog_row, e2]
    if h0 is not None:
        in_specs.append(pl.BlockSpec((1, 2, hp, N_A), lambda b, i: (b, 0, 0, 0)))
        args.append(h0)
    out_shape = [jax.ShapeDtypeStruct((n, hp), F32), jax.ShapeDtypeStruct((n, hp), F32)]
    out_specs = [pl.BlockSpec((CHUNK, hp), lambda b, i: (fwd(b, i), 0)),
                 pl.BlockSpec((CHUNK, hp), lambda b, i: (bwd(b, i), 0))]
    if emit_final:
        out_shape.append(jax.ShapeDtypeStruct((n_batch, 2, hp, N_A), F32))
        out_specs.append(pl.BlockSpec((1, 2, hp, N_A), lambda b, i: (b, 0, 0, 0)))
    kern = functools.partial(_ssd_kernel, has_h0=h0 is not None, emit_final=emit_final)
    return pl.pallas_call(
        kern, out_shape=tuple(out_shape), grid=(n_batch, nc),
        in_specs=in_specs, out_specs=tuple(out_specs),
        scratch_shapes=[pltpu.VMEM((2, N_A, hp), F32)],
        compiler_params=_cparams(2),
        name="ssd",
    )(*args)


def _lru_kernel(*refs, chain, emit_final):
    (xr_ref, gg_ref, wl_ref, lb_ref, lam_ref), rest = refs[:5], refs[5:]
    if chain:
        h0_ref, rest = rest[0], rest[1:]
    y_ref, rest = rest[0], rest[1:]
    if emit_final:
        fin_ref, rest = rest[0], rest[1:]
    a_scr, u_scr, h_scr, p_scr = rest
    rows = xr_ref.shape[0]
    n_seg = rows // ROW_CHUNK
    sp = _softplus(-lam_ref[0])

    def gates(r, carry):
        r0 = pl.multiple_of(r * ROW_CHUNK, ROW_CHUNK)
        s0 = pl.multiple_of(r * SEG_PITCH, 8)
        xr_b = xr_ref[pl.ds(r0, ROW_CHUNK), :]
        xr = xr_b.astype(F32)
        g = _dot(xr_b, wl_ref[0]) + lb_ref[0]
        for d in range(2):
            r_gate = _sigmoid(g[:, (2 * d) * BW_B:(2 * d + 1) * BW_B])
            i_gate = _sigmoid(g[:, (2 * d + 1) * BW_B:(2 * d + 2) * BW_B])
            a = jnp.exp((-LRU_C) * r_gate * sp[d:d + 1, :])
            u = jnp.sqrt(jnp.maximum(1.0 - a * a, 0.0)) * (i_gate * xr)
            a_scr[d, pl.ds(s0, ROW_CHUNK), :] = a
            u_scr[d, pl.ds(s0, ROW_CHUNK), :] = u
        return carry

    lax.fori_loop(0, n_seg, gates, 0)

    def step(t, carry):
        hf, hb, pf, pb = carry
        tb = ROW_CHUNK - 1 - t
        sf = pl.ds(t, n_seg, stride=SEG_PITCH)
        sb = pl.ds(tb, n_seg, stride=SEG_PITCH)
        af = a_scr[0, sf, :]
        ab = a_scr[1, sb, :]
        hf = af * hf + u_scr[0, sf, :]
        hb = ab * hb + u_scr[1, sb, :]
        h_scr[0, sf, :] = hf
        h_scr[1, sb, :] = hb
        if chain:
            pf = af * pf
            pb = ab * pb
            p_scr[0, sf, :] = pf
            p_scr[1, sb, :] = pb
        return hf, hb, pf, pb

    z = jnp.zeros((n_seg, BW_B), F32)
    o = jnp.ones((n_seg, BW_B), F32)
    lax.fori_loop(0, ROW_CHUNK, step, (z, z, o, o), unroll=8)

    if emit_final:
        fin_ref[0, 0, 0] = h_scr[0, pl.ds(ROW_CHUNK - 1, n_seg, stride=SEG_PITCH), :]
        fin_ref[0, 0, 1] = h_scr[1, pl.ds(0, n_seg, stride=SEG_PITCH), :]

    if chain:
        carry = h0_ref[0, 0, 0:1, :]
        for s in range(n_seg):
            sl = slice(s * SEG_PITCH, s * SEG_PITCH + ROW_CHUNK)
            h = h_scr[0, sl, :] + p_scr[0, sl, :] * carry
            h_scr[0, sl, :] = h
            carry = h[ROW_CHUNK - 1:ROW_CHUNK, :]
        carry = h0_ref[0, 0, 1:2, :]
        for s in reversed(range(n_seg)):
            sl = slice(s * SEG_PITCH, s * SEG_PITCH + ROW_CHUNK)
            h = h_scr[1, sl, :] + p_scr[1, sl, :] * carry
            h_scr[1, sl, :] = h
            carry = h[0:1, :]

    for s in range(n_seg):
        sl = slice(s * SEG_PITCH, s * SEG_PITCH + ROW_CHUNK)
        gg = gg_ref[s * ROW_CHUNK:(s + 1) * ROW_CHUNK, :].astype(F32)
        y_ref[s * ROW_CHUNK:(s + 1) * ROW_CHUNK, :] = (gg * (h_scr[0, sl, :] + h_scr[1, sl, :])).astype(BF16)


def _lru(p, wl, lb, lam, h0, chain, emit_final):
    n = p.shape[0]
    nblk = n // ROW_BLOCK
    n_seg = ROW_BLOCK // ROW_CHUNK
    xb_blk, gate_blk = D_MODEL // BW_B, 2 * D_MODEL // BW_B
    in_specs = [pl.BlockSpec((ROW_BLOCK, BW_B), lambda i, h: (i, xb_blk + h)),
                pl.BlockSpec((ROW_BLOCK, BW_B), lambda i, h: (i, gate_blk + h)),
                pl.BlockSpec((1, BW_B, 4 * BW_B), lambda i, h: (h, 0, 0)),
                pl.BlockSpec((1, 1, 4 * BW_B), lambda i, h: (h, 0, 0)),
                pl.BlockSpec((1, 2, BW_B), lambda i, h: (h, 0, 0))]
    args = [p, p, wl, lb, lam]
    if chain:
        in_specs.append(pl.BlockSpec((1, 1, 2, BW_B), lambda i, h: (i, h, 0, 0)))
        args.append(h0)
    out_shape = [jax.ShapeDtypeStruct((n, D_MODEL), BF16)]
    out_specs = [pl.BlockSpec((ROW_BLOCK, BW_B), lambda i, h: (i, h))]
    if emit_final:
        out_shape.append(jax.ShapeDtypeStruct((nblk, H_B, 2, n_seg, BW_B), F32))
        out_specs.append(pl.BlockSpec((1, 1, 2, n_seg, BW_B), lambda i, h: (i, h, 0, 0, 0)))
    seg_rows = n_seg * SEG_PITCH
    kern = functools.partial(_lru_kernel, chain=chain, emit_final=emit_final)
    return pl.pallas_call(
        kern, out_shape=tuple(out_shape), grid=(nblk, H_B),
        in_specs=in_specs, out_specs=tuple(out_specs),
        scratch_shapes=[pltpu.VMEM((2, seg_rows, BW_B), F32)] * 4,
        compiler_params=_cparams(2),
        name="lru",
    )(*args)


def _ev_out_kernel(x_ref, yf_ref, yb_ref, xs_ref, z_ref, yl_ref, wo_ref, dsk_ref, gn_ref, mod_ref, o_ref):
    y = yf_ref[...] + yb_ref[...] + dsk_ref[...] * xs_ref[...].astype(F32)
    y = y * _silu(z_ref[...].astype(F32))
    gw = D_MODEL // G_A
    parts = []
    for g in range(G_A):
        yg = y[:, g * gw:(g + 1) * gw]
        ms = jnp.mean(yg * yg, axis=-1, keepdims=True)
        parts.append((yg * lax.rsqrt(ms + EPS) * gn_ref[:, g * gw:(g + 1) * gw]).astype(BF16))
    mix = _dot(parts[0], wo_ref[0:gw, :]) + _dot(parts[1], wo_ref[gw:2 * gw, :])
    mix = mix + _dot(yl_ref[...], wo_ref[D_MODEL:2 * D_MODEL, :])
    o_ref[...] = x_ref[...] + mod_ref[0, 2:3, :] * mix


def _ev_out(x2d, yf, yb, p, ylru, wo, dsk, gn, mod_l, cond_of_tile):
    n = x2d.shape[0]
    row = lambda i: (i, 0)
    return pl.pallas_call(
        _ev_out_kernel,
        out_shape=jax.ShapeDtypeStruct((n, D_MODEL), F32),
        grid=(n // TM,),
        in_specs=[pl.BlockSpec((TM, D_MODEL), row), pl.BlockSpec((TM, D_MODEL), row),
                  pl.BlockSpec((TM, D_MODEL), row),
                  pl.BlockSpec((TM, D_MODEL), lambda i: (i, 3)),
                  pl.BlockSpec((TM, D_MODEL), lambda i: (i, 0)),
                  pl.BlockSpec((TM, D_MODEL), row),
                  _const_spec((2 * D_MODEL, D_MODEL)), _const_spec((1, D_MODEL)), _const_spec((1, D_MODEL)),
                  pl.BlockSpec((1, 6, D_MODEL), lambda i: (cond_of_tile(i), 0, 0))],
        out_specs=pl.BlockSpec((TM, D_MODEL), row),
        compiler_params=_cparams(1),
        name="ev_out",
    )(x2d, yf, yb, p, p, ylru, wo, dsk, gn, mod_l)


def _ffn_kernel(x_ref, mod_ref, g_ref, wgu_ref, wd_ref, o_ref):
    x = x_ref[...]
    h = _modulated_norm(x, g_ref[...], mod_ref[0, 4:5, :], mod_ref[0, 3:4, :]).astype(BF16)
    n_chunk = 2
    fc = D_FF // n_chunk
    f = None
    for c in range(n_chunk):
        g = _dot(h, wgu_ref[:, c * fc:(c + 1) * fc])
        u = _dot(h, wgu_ref[:, D_FF + c * fc:D_FF + (c + 1) * fc])
        part = _dot((_silu(g) * u).astype(BF16), wd_ref[c * fc:(c + 1) * fc, :])
        f = part if f is None else f + part
    o_ref[...] = x + mod_ref[0, 5:6, :] * f


def _ffn(x2d, mod_l, cond_of_tile, gain, wgu, wd):
    n = x2d.shape[0]
    return pl.pallas_call(
        _ffn_kernel,
        out_shape=jax.ShapeDtypeStruct((n, D_MODEL), F32),
        grid=(n // TM,),
        in_specs=[pl.BlockSpec((TM, D_MODEL), lambda i: (i, 0)),
                  pl.BlockSpec((1, 6, D_MODEL), lambda i: (cond_of_tile(i), 0, 0)),
                  _const_spec((1, D_MODEL)), _const_spec((D_MODEL, 2 * D_FF)), _const_spec((D_FF, D_MODEL))],
        out_specs=pl.BlockSpec((TM, D_MODEL), lambda i: (i, 0)),
        compiler_params=_cparams(1),
        name="ffn",
    )(x2d, mod_l, gain, wgu, wd)


OD_IN_COLS = 896


def _od_inproj_kernel(*refs, use_rope):
    (x_ref, mod_ref, g_ref, wi_ref, qn_ref, kvn_ref, wq_ref), rest = refs[:7], refs[7:]
    if use_rope:
        (wqs_ref, cq_ref, sq_ref, ck_ref, sk_ref), rest = rest[:5], rest[5:]
    q_ref, ckv_ref, kr_ref = rest
    h = _modulated_norm(x_ref[...], g_ref[...], mod_ref[0, 1:2, :], mod_ref[0, 0:1, :]).astype(BF16)
    proj = _dot(h, wi_ref[...])
    cq = proj[:, 0:Q_RANK]
    cqn = (cq * lax.rsqrt(jnp.mean(cq * cq, axis=-1, keepdims=True) + EPS) * qn_ref[...]).astype(BF16)
    ckv = proj[:, Q_RANK:Q_RANK + KV_RANK]
    ckv_ref[...] = ckv * lax.rsqrt(jnp.mean(ckv * ckv, axis=-1, keepdims=True) + EPS) * kvn_ref[...]
    kr = proj[:, 640:640 + ROPE_C]
    q = _dot(cqn, wq_ref[...])
    if use_rope:
        q = q * jnp.tile(cq_ref[...], (1, H_C)) + _dot(cqn, wqs_ref[...]) * jnp.tile(sq_ref[...], (1, H_C))
        kr = kr * ck_ref[...] + proj[:, 768:768 + ROPE_C] * sk_ref[...]
    q = (q * (NOPE_C + ROPE_C) ** -0.5).astype(BF16)
    for h in range(H_C):
        q_ref[h] = q[:, h * HEAD_PAD:(h + 1) * HEAD_PAD]
    kr_ref[...] = kr


def _od_inproj(x2d, mod_l, cond_of_tile, gain, wi, qn, kvn, wq, rope, seq_len):
    n = x2d.shape[0]
    use_rope = rope is not None
    in_specs = [pl.BlockSpec((TM, D_MODEL), lambda i: (i, 0)),
                pl.BlockSpec((1, 6, D_MODEL), lambda i: (cond_of_tile(i), 0, 0)),
                _const_spec((1, D_MODEL)), _const_spec((D_MODEL, OD_IN_COLS)),
                _const_spec((1, Q_RANK)), _const_spec((1, KV_RANK)),
                _const_spec((Q_RANK, H_C * HEAD_PAD))]
    args = [x2d, mod_l, gain, wi, qn, kvn, wq]
    if use_rope:
        wqs, cos_q, sin_q, cos_k, sin_k = rope
        per_seq = seq_len // TM
        pos = lambda i: (i % per_seq, 0)
        in_specs += [_const_spec((Q_RANK, H_C * HEAD_PAD)),
                     pl.BlockSpec((TM, HEAD_PAD), pos), pl.BlockSpec((TM, HEAD_PAD), pos),
                     pl.BlockSpec((TM, ROPE_C), pos), pl.BlockSpec((TM, ROPE_C), pos)]
        args += [wqs, cos_q, sin_q, cos_k, sin_k]
    return pl.pallas_call(
        functools.partial(_od_inproj_kernel, use_rope=use_rope),
        out_shape=(jax.ShapeDtypeStruct((H_C, n, HEAD_PAD), BF16),
                   jax.ShapeDtypeStruct((n, KV_RANK), F32),
                   jax.ShapeDtypeStruct((n, ROPE_C), F32)),
        grid=(n // TM,),
        in_specs=in_specs,
        out_specs=(pl.BlockSpec((H_C, TM, HEAD_PAD), lambda i: (0, i, 0)),
                   pl.BlockSpec((TM, KV_RANK), lambda i: (i, 0)),
                   pl.BlockSpec((TM, ROPE_C), lambda i: (i, 0))),
        compiler_params=_cparams(1),
        name="od_inproj",
    )(*args)


def _kv_up_kernel(ckv_ref, kr_ref, wk_ref, ek_ref, wv_ref, k_ref, v_ref):
    ckv = ckv_ref[...].astype(BF16)
    k = (_dot(ckv, wk_ref[...]) + _dot(kr_ref[...].astype(BF16), ek_ref[...])).astype(BF16)
    v = _dot(ckv, wv_ref[...]).astype(BF16)
    for h in range(H_C):
        k_ref[h] = k[:, h * HEAD_PAD:(h + 1) * HEAD_PAD]
    for hp in range(H_C // 2):
        v_ref[hp] = v[:, hp * LANES:(hp + 1) * LANES]


def _kv_up(ckv_all, kr_all, wk, ek, wv):
    m = ckv_all.shape[0]
    return pl.pallas_call(
        _kv_up_kernel,
        out_shape=(jax.ShapeDtypeStruct((H_C, m, HEAD_PAD), BF16),
                   jax.ShapeDtypeStruct((H_C // 2, m, LANES), BF16)),
        grid=(m // TM,),
        in_specs=[pl.BlockSpec((TM, KV_RANK), lambda i: (i, 0)), pl.BlockSpec((TM, ROPE_C), lambda i: (i, 0)),
                  _const_spec((KV_RANK, H_C * HEAD_PAD)), _const_spec((ROPE_C, H_C * HEAD_PAD)),
                  _const_spec((KV_RANK, H_C * V_C))],
        out_specs=(pl.BlockSpec((H_C, TM, HEAD_PAD), lambda i: (0, i, 0)),
                   pl.BlockSpec((H_C // 2, TM, LANES), lambda i: (0, i, 0))),
        compiler_params=_cparams(1),
        name="kv_up",
    )(ckv_all, kr_all, wk, ek, wv)


def _attn_kernel(q_ref, k_ref, v_ref, o_ref):
    lane = lax.broadcasted_iota(jnp.int32, (q_ref.shape[1], LANES), 1)

    def head_pair(hp, carry):
        outs = []
        for side in range(2):
            h = 2 * hp + side
            s = lax.dot_general(q_ref[h], k_ref[h], (((1,), (1,)), ((), ())), preferred_element_type=F32)
            p = jnp.exp(s - jnp.max(s, axis=-1, keepdims=True))
            l = jnp.sum(p, axis=-1, keepdims=True)
            outs.append(_dot(p.astype(BF16), v_ref[hp]) / l)
        o_ref[hp] = jnp.where(lane < V_C, outs[0], outs[1]).astype(BF16)
        return carry

    lax.fori_loop(0, H_C // 2, head_pair, 0)


def _attention(q, k, v, n_batch, seq_len, kv_len):
    tq = 256
    per = seq_len // tq
    return pl.pallas_call(
        _attn_kernel,
        out_shape=jax.ShapeDtypeStruct((H_C // 2, q.shape[1], LANES), BF16),
        grid=(n_batch, per),
        in_specs=[pl.BlockSpec((H_C, tq, HEAD_PAD), lambda b, i: (0, b * per + i, 0)),
                  pl.BlockSpec((H_C, kv_len, HEAD_PAD), lambda b, i: (0, b, 0)),
                  pl.BlockSpec((H_C // 2, kv_len, LANES), lambda b, i: (0, b, 0))],
        out_specs=pl.BlockSpec((H_C // 2, tq, LANES), lambda b, i: (0, b * per + i, 0)),
        compiler_params=_cparams(2),
        name="attention",
    )(q, k, v)


TOKEN_TILE = (8, LANES)


def _store_token_tiles(ref, value):
    for s in range(TOKEN_TILE[0]):
        ref[:, s, :] = value[:, s * LANES:(s + 1) * LANES]


def _load_token_tiles(ref):
    return jnp.concatenate([ref[:, s, :] for s in range(TOKEN_TILE[0])], axis=-1)


ROUTE_IDX0, ROUTE_IDX1, ROUTE_RANK0, ROUTE_RANK1, ROUTE_W0, ROUTE_W1 = range(6)


def _od_out_router_kernel(o_ref, x_ref, wo_ref, mod_ref, g_ref, wr_ref, x3_ref, h2_ref, route_ref, cnt_ref, cnt_scr):
    i = pl.program_id(0)

    @pl.when(i == 0)
    def _():
        cnt_scr[...] = jnp.zeros(cnt_scr.shape, F32)

    o = jnp.concatenate([o_ref[hp] for hp in range(H_C // 2)], axis=-1)
    x3 = x_ref[...] + mod_ref[0, 2:3, :] * _dot(o, wo_ref[...])
    x3_ref[...] = x3
    h2 = _modulated_norm(x3, g_ref[...], mod_ref[0, 4:5, :], mod_ref[0, 3:4, :])
    _store_token_tiles(h2_ref, h2)
    logits = jnp.dot(h2, wr_ref[...], precision=HIGHEST, preferred_element_type=F32)
    tm = logits.shape[0]
    lane = lax.broadcasted_iota(jnp.int32, logits.shape, 1)
    neg = jnp.float32(-jnp.inf)
    logits = jnp.where(lane < N_EXPERTS, logits, neg)
    m0 = jnp.max(logits, axis=-1, keepdims=True)
    lane_f = lane.astype(F32)
    i0 = jnp.min(jnp.where(logits == m0, lane_f, float(LANES)), axis=-1, keepdims=True)
    rest = jnp.where(lane_f == i0, neg, logits)
    m1 = jnp.max(rest, axis=-1, keepdims=True)
    i1 = jnp.min(jnp.where(rest == m1, lane_f, float(LANES)), axis=-1, keepdims=True)
    e = jnp.exp(m1 - m0)
    w0 = 1.0 / (1.0 + e)
    w1 = e * w0
    oh0 = jnp.where(lane_f == i0, 1.0, 0.0).astype(F32)
    oh1 = jnp.where(lane_f == i1, 1.0, 0.0).astype(F32)
    sel = oh0 + oh1
    ri = lax.broadcasted_iota(jnp.int32, (tm, tm), 0)
    ci = lax.broadcasted_iota(jnp.int32, (tm, tm), 1)
    before = jnp.where(ci < ri, 1.0, 0.0).astype(BF16)
    prior = _dot(before, sel.astype(BF16)) + cnt_scr[0:1, :]
    r0 = jnp.sum(oh0 * prior, axis=-1, keepdims=True)
    r1 = jnp.sum(oh1 * prior, axis=-1, keepdims=True)
    route = jnp.where(lane == ROUTE_IDX0, i0, 0.0)
    route = jnp.where(lane == ROUTE_IDX1, i1, route)
    route = jnp.where(lane == ROUTE_RANK0, r0, route)
    route = jnp.where(lane == ROUTE_RANK1, r1, route)
    route = jnp.where(lane == ROUTE_W0, w0, route)
    route = jnp.where(lane == ROUTE_W1, w1, route)
    route_ref[...] = route
    total = cnt_scr[0:1, :] + jnp.sum(sel, axis=0, keepdims=True)
    cnt_scr[...] = jnp.broadcast_to(total, cnt_scr.shape)
    cnt_ref[...] = cnt_scr[...]


def _od_out_router(o, x2d, wo, mod_l, cond_of_tile, gain, wr):
    n = x2d.shape[0]
    row = lambda i: (i, 0)
    return pl.pallas_call(
        _od_out_router_kernel,
        out_shape=(jax.ShapeDtypeStruct((n, D_MODEL), F32), jax.ShapeDtypeStruct((n,) + TOKEN_TILE, F32),
                   jax.ShapeDtypeStruct((n, LANES), F32), jax.ShapeDtypeStruct((8, LANES), F32)),
        grid=(n // TM,),
        in_specs=[pl.BlockSpec((H_C // 2, TM, LANES), lambda i: (0, i, 0)), pl.BlockSpec((TM, D_MODEL), row),
                  _const_spec((D_MODEL, D_MODEL)),
                  pl.BlockSpec((1, 6, D_MODEL), lambda i: (cond_of_tile(i), 0, 0)),
                  _const_spec((1, D_MODEL)), _const_spec((D_MODEL, LANES))],
        out_specs=(pl.BlockSpec((TM, D_MODEL), row), pl.BlockSpec((TM,) + TOKEN_TILE, lambda i: (i, 0, 0)),
                   pl.BlockSpec((TM, LANES), row), pl.BlockSpec((8, LANES), lambda i: (0, 0))),
        scratch_shapes=[pltpu.VMEM((8, LANES), F32)],
        compiler_params=_cparams(1),
        name="od_out_router",
    )(o, x2d, wo, mod_l, gain, wr)


def _row_copy(src, s, dst, d, sem):
    return pltpu.make_async_copy(src.at[pl.ds(s, 1)], dst.at[pl.ds(d, 1)], sem)


def _experts_kernel(te_ref, nt_ref, nv_ref, src0_ref, src1_ref, dst_ref, h_hbm, wg_ref, wu_ref, wd_ref, y_hbm,
                    h_buf, y_buf, h_scr, acc_scr, sem_in, sem_out):
    i, c = pl.program_id(0), pl.program_id(1)
    last_c = pl.num_programs(1) - 1
    n_tiles = nt_ref[0]

    def gather(tile, src_ref):
        buf = tile % 2

        def body(r, carry):
            _row_copy(h_hbm, src_ref[r], h_buf.at[buf], r, sem_in.at[buf]).start()
            return carry
        lax.fori_loop(0, nv_ref[tile], body, 0)

    def rows_copy(src, dst, n, sem):
        return pltpu.make_async_copy(src.at[pl.ds(0, n)], dst.at[pl.ds(0, n)], sem)

    @pl.when(i < n_tiles)
    def _():
        @pl.when(c == 0)
        def _():
            @pl.when(i == 0)
            def _():
                h_buf[...] = jnp.zeros(h_buf.shape, F32)
                gather(0, src0_ref)

            @pl.when(i + 1 < n_tiles)
            def _():
                gather(i + 1, src1_ref)

            buf = i % 2
            rows_copy(h_hbm, h_buf.at[buf], nv_ref[i], sem_in.at[buf]).wait()
            h_scr[...] = _load_token_tiles(h_buf.at[buf]).astype(BF16)

        h = h_scr[...]
        g = _dot(h, wg_ref[0])
        u = _dot(h, wu_ref[0])
        part = _dot((_silu(g) * u).astype(BF16), wd_ref[0])

        @pl.when(c == 0)
        def _():
            acc_scr[...] = part

        @pl.when((c > 0) & (c < last_c))
        def _():
            acc_scr[...] += part

        @pl.when(c == last_c)
        def _():
            @pl.when(i > 0)
            def _():
                rows_copy(y_buf, y_hbm, nv_ref[i - 1], sem_out).wait()
            _store_token_tiles(y_buf, acc_scr[...] + part)

            def body(r, carry):
                _row_copy(y_buf, r, y_hbm, dst_ref[r], sem_out).start()
                return carry
            lax.fori_loop(0, nv_ref[i], body, 0)

            @pl.when(i == n_tiles - 1)
            def _():
                rows_copy(y_buf, y_hbm, nv_ref[i], sem_out).wait()


def _experts(tile_expert, n_tiles, n_valid, slot_src, slot_dst, h_all, wgu, wd, n_out_rows):
    rows = slot_src.shape[0]
    t = rows // TM_E
    n_c = D_FF_E // FF_CHUNK_E
    assert n_c > 1
    live = lambda i, nt: jnp.minimum(i, nt[0] - 1)
    smem_rows = lambda f: pl.BlockSpec((TM_E,), f, memory_space=pltpu.SMEM)
    grid_spec = pltpu.PrefetchScalarGridSpec(
        num_scalar_prefetch=3,
        grid=(t, n_c),
        in_specs=[smem_rows(lambda i, c, te, nt, nv: (0,)),
                  smem_rows(lambda i, c, te, nt, nv: (jnp.minimum(i + 1, t - 1),)),
                  smem_rows(lambda i, c, te, nt, nv: (i,)),
                  pl.BlockSpec(memory_space=pl.ANY),
                  pl.BlockSpec((1, D_MODEL, FF_CHUNK_E), lambda i, c, te, nt, nv: (te[live(i, nt)], 0, c)),
                  pl.BlockSpec((1, D_MODEL, FF_CHUNK_E), lambda i, c, te, nt, nv: (te[live(i, nt)], 0, n_c + c)),
                  pl.BlockSpec((1, FF_CHUNK_E, D_MODEL), lambda i, c, te, nt, nv: (te[live(i, nt)], c, 0))],
        out_specs=pl.BlockSpec(memory_space=pl.ANY),
        scratch_shapes=[pltpu.VMEM((2, TM_E) + TOKEN_TILE, F32), pltpu.VMEM((TM_E,) + TOKEN_TILE, F32),
                        pltpu.VMEM((TM_E, D_MODEL), BF16), pltpu.VMEM((TM_E, D_MODEL), F32),
                        pltpu.SemaphoreType.DMA((2,)), pltpu.SemaphoreType.DMA(())],
    )
    return pl.pallas_call(
        _experts_kernel,
        out_shape=jax.ShapeDtypeStruct((n_out_rows,) + TOKEN_TILE, F32),
        grid_spec=grid_spec,
        compiler_params=pltpu.CompilerParams(dimension_semantics=("arbitrary", "arbitrary"),
                                             vmem_limit_bytes=VMEM_LIMIT, has_side_effects=True),
        name="experts",
    )(tile_expert, n_tiles, n_valid, slot_src, slot_src, slot_dst, h_all, wgu, wgu, wd)


def _combine_kernel(x_ref, ya_ref, yb_ref, route_ref, mod_ref, g_ref, o_ref):
    w0 = route_ref[:, ROUTE_W0:ROUTE_W0 + 1]
    w1 = route_ref[:, ROUTE_W1:ROUTE_W1 + 1]
    x = x_ref[...] + mod_ref[0, 5:6, :] * (w0 * _load_token_tiles(ya_ref) + w1 * _load_token_tiles(yb_ref))
    o_ref[...] = x * lax.rsqrt(jnp.mean(x * x, axis=-1, keepdims=True) + EPS) * g_ref[...]


def _combine(x3, y2, first_row, route, mod_l, cond_of_tile, final_g):
    n = x3.shape[0]
    nt = n // TM
    blk0 = first_row // TM
    blk1 = (y2.shape[0] // 2 + first_row) // TM
    return pl.pallas_call(
        _combine_kernel,
        out_shape=jax.ShapeDtypeStruct((n, D_MODEL), F32),
        grid=(nt,),
        in_specs=[pl.BlockSpec((TM, D_MODEL), lambda i: (i, 0)),
                  pl.BlockSpec((TM,) + TOKEN_TILE, lambda i: (blk0 + i, 0, 0)),
                  pl.BlockSpec((TM,) + TOKEN_TILE, lambda i: (blk1 + i, 0, 0)),
                  pl.BlockSpec((TM, LANES), lambda i: (i, 0)),
                  pl.BlockSpec((1, 6, D_MODEL), lambda i: (cond_of_tile(i), 0, 0)),
                  _const_spec((1, D_MODEL))],
        out_specs=pl.BlockSpec((TM, D_MODEL), lambda i: (i, 0)),
        compiler_params=_cparams(1),
        name="combine",
    )(x3, y2, y2, route, mod_l, final_g)


def _even_params(ev_w_in, ev_conv_a_w, ev_conv_a_b, ev_a_log, ev_dt_bias, ev_d_skip, ev_gnorm,
                 ev_conv_b_w, ev_conv_b_b, ev_lru_w, ev_lru_b, ev_lru_lam, ev_w_out, ev_ffn_gu, ev_ffn_down):
    d_a = H_A * P_A
    xbc = d_a + 2 * G_A * N_A
    o_xbc, o_dt, o_gate, o_xb = d_a, d_a + xbc, d_a + xbc + 2 * H_A, d_a + xbc + 2 * H_A + D_MODEL
    w = ev_w_in
    w_main = jnp.concatenate([w[:, :d_a], w[:, o_xb:], w[:, o_gate:o_xb], w[:, o_xbc:o_dt]], axis=1).astype(BF16)
    w_dt = jnp.pad(w[:, o_dt:o_gate], ((0, 0), (0, LANES - 2 * H_A))).astype(BF16)
    zeros_w = jnp.zeros((CONV_K, D_MODEL), F32)
    cw = jnp.concatenate([zeros_w, ev_conv_b_w, zeros_w, ev_conv_a_w], axis=1)
    zeros_b = jnp.zeros((D_MODEL,), F32)
    cb = jnp.concatenate([zeros_b, ev_conv_b_b, zeros_b, ev_conv_a_b])[None, :]
    dtb = jnp.pad(ev_dt_bias.reshape(-1), (0, LANES - 2 * H_A))[None, :]
    alog = jnp.pad(ev_a_log.reshape(-1), (0, LANES - 2 * H_A))[None, :]
    j = jnp.arange(2 * LANES) % LANES
    c = jnp.arange(d_a) // P_A
    e2 = jnp.stack([(j[:, None] == (d * H_A + c)[None, :]) for d in range(2)]).astype(BF16)
    wl = jnp.transpose(ev_lru_w, (2, 3, 0, 1, 4)).reshape(H_B, BW_B, 4 * BW_B).astype(BF16)
    lb = jnp.transpose(ev_lru_b.reshape(2, 2, H_B, BW_B), (2, 0, 1, 3)).reshape(H_B, 1, 4 * BW_B)
    lam = jnp.transpose(ev_lru_lam.reshape(2, H_B, BW_B), (1, 0, 2))
    dsk = jnp.repeat(ev_d_skip, P_A)[None, :]
    return dict(w_main=w_main, w_dt=w_dt, cw=cw, cb=cb, dtb=dtb, alog=alog, e2=e2, wl=wl, lb=lb, lam=lam,
                dsk=dsk, gn=ev_gnorm[None, :], w_out=ev_w_out.astype(BF16),
                wgu=ev_ffn_gu.astype(BF16), wd=ev_ffn_down.astype(BF16))


def _rope_partner(w, lo):
    blk = w[:, lo:lo + ROPE_C].reshape(w.shape[0], 2, 2, N_FREQ)
    return jnp.flip(blk, axis=2).reshape(w.shape[0], ROPE_C)


def _odd_params(od_w_in, od_q_norm, od_w_q_up, od_kv_norm, od_w_kv_up, od_w_o, od_router):
    kr_lo = Q_RANK + KV_RANK
    wi = jnp.zeros((D_MODEL, OD_IN_COLS), F32)
    wi = wi.at[:, :kr_lo + ROPE_C].set(od_w_in)
    wi = wi.at[:, 768:768 + ROPE_C].set(_rope_partner(od_w_in, kr_lo))
    wq3 = od_w_q_up.reshape(Q_RANK, H_C, NOPE_C + ROPE_C)
    pad = HEAD_PAD - NOPE_C - ROPE_C
    wq = jnp.pad(wq3, ((0, 0), (0, 0), (0, pad))).reshape(Q_RANK, H_C * HEAD_PAD)
    partner = jnp.flip(wq3[:, :, NOPE_C:].reshape(Q_RANK, H_C, 2, 2, N_FREQ), axis=3).reshape(Q_RANK, H_C, ROPE_C)
    wqs = jnp.pad(partner, ((0, 0), (0, 0), (NOPE_C, pad))).reshape(Q_RANK, H_C * HEAD_PAD)
    wkv3 = od_w_kv_up.reshape(KV_RANK, H_C, NOPE_C + V_C)
    wk = jnp.pad(wkv3[:, :, :NOPE_C], ((0, 0), (0, 0), (0, HEAD_PAD - NOPE_C))).reshape(KV_RANK, H_C * HEAD_PAD)
    wv = wkv3[:, :, NOPE_C:].reshape(KV_RANK, H_C * V_C)
    ek_head = jnp.pad(jnp.eye(ROPE_C, dtype=F32), ((0, 0), (NOPE_C, pad)))
    ek = jnp.tile(ek_head, (1, H_C))
    wr = jnp.pad(od_router, ((0, 0), (0, LANES - N_EXPERTS)))
    return dict(wi=wi.astype(BF16), qn=od_q_norm[None, :], kvn=od_kv_norm[None, :], wq=wq.astype(BF16),
                wqs=wqs.astype(BF16), wk=wk.astype(BF16), ek=ek.astype(BF16), wv=wv.astype(BF16),
                wo=od_w_o.astype(BF16), wr=wr)


def _rope_tables(n_tokens):
    rows = n_tokens // GRID_W
    row = jnp.repeat(jnp.arange(rows), GRID_W).astype(F32)
    col = jnp.tile(jnp.arange(GRID_W), rows).astype(F32)
    inv = ROPE_BASE ** (-jnp.arange(N_FREQ, dtype=F32) / N_FREQ)
    ang_r, ang_c = row[:, None] * inv, col[:, None] * inv
    cos_k = jnp.concatenate([jnp.cos(ang_r)] * 2 + [jnp.cos(ang_c)] * 2, axis=1)
    sin_k = jnp.concatenate([-jnp.sin(ang_r), jnp.sin(ang_r), -jnp.sin(ang_c), jnp.sin(ang_c)], axis=1)
    pad = HEAD_PAD - NOPE_C - ROPE_C
    cos_q = jnp.pad(cos_k, ((0, 0), (NOPE_C, pad)), constant_values=1.0)
    sin_q = jnp.pad(sin_k, ((0, 0), (NOPE_C, pad)))
    return cos_q, sin_q, cos_k, sin_k


def _layer0(x2d, mod_l, cond_row, norm_g0, ev, n_batch, seq_len, h0_ssd, h0_lru, is_ctx):
    per_block = max(ROW_BLOCK // seq_len, 1)
    blocks_per_seq = max(seq_len // ROW_BLOCK, 1)
    cond_of_block = (lambda i: 0) if is_ctx else (lambda i: cond_row(i // blocks_per_seq))
    tiles_per_seq = seq_len // TM if seq_len >= TM else 1
    cond_of_tile = (lambda i: 0) if is_ctx else (lambda i: cond_row(i // tiles_per_seq))
    p, dt = _ev_inproj(x2d, mod_l, cond_of_block, norm_g0[0:1], ev["w_main"], ev["w_dt"], ev["cw"], ev["cb"],
                       ev["dtb"], seq_len)
    ssd_out = _ssd(p, dt, ev["alog"], ev["e2"], h0_ssd, n_batch, seq_len, emit_final=is_ctx)
    lru_out = _lru(p, ev["wl"], ev["lb"], ev["lam"], h0_lru, chain=not is_ctx, emit_final=is_ctx)
    x1 = _ev_out(x2d, ssd_out[0], ssd_out[1], p, lru_out[0], ev["w_out"], ev["dsk"], ev["gn"], mod_l, cond_of_tile)
    x2 = _ffn(x1, mod_l, cond_of_tile, norm_g0[1:2], ev["wgu"], ev["wd"])
    finals = None
    if is_ctx:
        hp = H_A * P_A
        s_ssd = ssd_out[2].reshape(n_batch, 1, 2, H_A, P_A, N_A)
        fin = lru_out[1]
        s_lru = jnp.transpose(fin, (0, 3, 2, 1, 4)).reshape(n_batch, 1, 2, H_B * BW_B)
        finals = (s_ssd, s_lru)
        del hp, per_block
    return x2, finals, cond_of_tile


def _layer1_pre(x2d, mod_l, cond_of_tile, norm_g1, od, n_batch, seq_len, rope, cache):
    q, ckv, kr = _od_inproj(x2d, mod_l, cond_of_tile, norm_g1[0:1], od["wi"], od["qn"], od["kvn"], od["wq"],
                            rope, seq_len)
    if cache is None:
        ckv_all, kr_all, kv_len = ckv, kr, seq_len
    else:
        c_ckv, c_kr = cache
        past = c_ckv.shape[1]
        kv_len = past + seq_len
        ckv_all = jnp.concatenate([c_ckv, ckv.reshape(n_batch, seq_len, KV_RANK)], axis=1).reshape(-1, KV_RANK)
        kr_all = jnp.concatenate([c_kr, kr.reshape(n_batch, seq_len, ROPE_C)], axis=1).reshape(-1, ROPE_C)
    k, v = _kv_up(ckv_all, kr_all, od["wk"], od["ek"], od["wv"])
    o = _attention(q, k, v, n_batch, seq_len, kv_len)
    x3, h2, route, counts = _od_out_router(o, x2d, od["wo"], mod_l, cond_of_tile, norm_g1[1:2], od["wr"])
    return x3, h2, route, counts, ckv, kr


def kernel(x_prompt, x_sample, state_ssd, state_lru, cache_ckv, cache_krope, c, c_ctx, mod_w, mod_b, norm_g, final_g, ev_w_in, ev_conv_a_w, ev_conv_a_b, ev_a_log, ev_dt_bias, ev_d_skip, ev_gnorm, ev_conv_b_w, ev_conv_b_b, ev_lru_w, ev_lru_b, ev_lru_lam, ev_w_out, ev_ffn_gu, ev_ffn_down, od_w_in, od_q_norm, od_w_q_up, od_kv_norm, od_w_kv_up, od_w_o, od_router, od_moe_gu, od_moe_down):
    batch, seq, _ = x_prompt.shape
    dbatch, dseq, _ = x_sample.shape
    n_ctx, n_dec = batch * seq, dbatch * dseq

    cond = jnp.concatenate([c_ctx[None, :], c, jnp.zeros((16 - 1 - dbatch, D_MODEL), F32)], axis=0)
    mod = _modulation(cond, mod_w, mod_b)
    ev = _even_params(ev_w_in[0], ev_conv_a_w[0], ev_conv_a_b[0], ev_a_log[0], ev_dt_bias[0], ev_d_skip[0],
                      ev_gnorm[0], ev_conv_b_w[0], ev_conv_b_b[0], ev_lru_w[0], ev_lru_b[0], ev_lru_lam[0],
                      ev_w_out[0], ev_ffn_gu[0], ev_ffn_down[0])
    od = _odd_params(od_w_in[0], od_q_norm[0], od_w_q_up[0], od_kv_norm[0], od_w_kv_up[0], od_w_o[0], od_router[0])
    wgu_e = od_moe_gu[0].astype(BF16)
    wd_e = od_moe_down[0].astype(BF16)
    dec_row = lambda b: 1 + b

    xc, finals, tile_c = _layer0(x_prompt.reshape(n_ctx, D_MODEL), mod[0], None, norm_g[0], ev, batch, seq,
                                 None, None, True)
    h0_ssd = state_ssd[:, 0].reshape(dbatch, 2, H_A * P_A, N_A)
    h0_lru = jnp.transpose(state_lru[:, 0].reshape(dbatch, 2, H_B, BW_B), (0, 2, 1, 3))
    xd, _, tile_d = _layer0(x_sample.reshape(n_dec, D_MODEL), mod[0], dec_row, norm_g[0], ev, dbatch, dseq,
                            h0_ssd, h0_lru, False)
    new_state_ssd, new_state_lru = finals

    rope = (od["wqs"],) + _rope_tables(dseq)
    x3c, h2c, route_c, cnt_c, ckv_c, kr_c = _layer1_pre(xc, mod[1], tile_c, norm_g[1], od, batch, seq, None, None)
    x3d, h2d, route_d, cnt_d, _, _ = _layer1_pre(xd, mod[1], tile_d, norm_g[1], od, dbatch, dseq, rope,
                                                  (cache_ckv[:, 0], cache_krope[:, 0]))
    new_cache_ckv = ckv_c.reshape(batch, 1, seq, KV_RANK)
    new_cache_krope = kr_c.reshape(batch, 1, seq, ROPE_C)

    y2 = _routed_experts(h2c, route_c, cnt_c, h2d, route_d, cnt_d, wgu_e, wd_e)
    y_prompt = _combine(x3c, y2, 0, route_c, mod[1], tile_c, final_g[None, :]).reshape(batch, seq, D_MODEL)
    y_sample = _combine(x3d, y2, n_ctx, route_d, mod[1], tile_d, final_g[None, :]).reshape(dbatch, dseq, D_MODEL)
    return (y_prompt, y_sample, new_state_ssd, new_state_lru, new_cache_ckv, new_cache_krope)


def _routed_experts(h2c, route_c, cnt_c, h2d, route_d, cnt_d, wgu_e, wd_e):
    n_ctx, n_dec = h2c.shape[0], h2d.shape[0]
    n_tok = n_ctx + n_dec
    n_tiles_max = 2 * n_tok // TM_E + N_EXPERTS
    cnt_c8 = cnt_c[0, :N_EXPERTS].astype(jnp.int32)
    cnt_d8 = cnt_d[0, :N_EXPERTS].astype(jnp.int32)
    cnt_e = cnt_c8 + cnt_d8
    tiles_e = (cnt_e + TM_E - 1) // TM_E
    tile_end = jnp.cumsum(tiles_e)
    tile_start = tile_end - tiles_e
    row_start = tile_start * TM_E
    n_tiles = tile_end[-1:].astype(jnp.int32)
    tile_id = jnp.arange(n_tiles_max, dtype=jnp.int32)
    tile_expert = jnp.minimum(jnp.sum((tile_end[None, :] <= tile_id[:, None]).astype(jnp.int32), axis=1),
                              N_EXPERTS - 1)
    n_valid = jnp.clip(cnt_e[tile_expert] - (tile_id - tile_start[tile_expert]) * TM_E, 0, TM_E).astype(jnp.int32)

    def positions(route, extra):
        idx = route[:, ROUTE_IDX0:ROUTE_IDX1 + 1].astype(jnp.int32)
        rank = route[:, ROUTE_RANK0:ROUTE_RANK1 + 1].astype(jnp.int32)
        return (row_start + extra)[idx] + rank

    pos = jnp.concatenate([positions(route_c, jnp.zeros_like(cnt_c8)), positions(route_d, cnt_c8)], axis=0)
    out_row = jnp.arange(n_tok, dtype=jnp.int32)[:, None] + jnp.array([0, n_tok], jnp.int32)[None, :]
    slot_dst = jnp.zeros((n_tiles_max * TM_E,), jnp.int32).at[pos.reshape(-1)].set(out_row.reshape(-1))
    slot_src = jnp.where(slot_dst >= n_tok, slot_dst - n_tok, slot_dst)
    h2 = jnp.concatenate([h2c, h2d], axis=0)
    return _experts(tile_expert, n_tiles, n_valid, slot_src, slot_dst, h2, wgu_e, wd_e, 2 * n_tok)
```

```python
import functools
import math

import jax
import jax.numpy as jnp
from jax import lax
from jax.experimental import pallas as pl
from jax.experimental.pallas import tpu as pltpu

F32 = jnp.float32
BF16 = jnp.bfloat16
HIGHEST = lax.Precision.HIGHEST

D_MODEL = 1024
GRID_W = 64
P_A = 64
H_A = 16
G_A = 2
N_A = 128
CHUNK = 128
CONV_K = 4
H_B = 8
BW_B = 128
LRU_C = 8.0
H_C = 16
Q_RANK = 384
KV_RANK = 256
NOPE_C = 64
ROPE_C = 32
V_C = 64
N_FREQ = ROPE_C // 4
ROPE_BASE = 10000.0
D_FF = 2816
N_EXPERTS = 8
D_FF_E = 3584
EPS = 1e-6

LANES = 128
HEAD_PAD = 128
ROW_BLOCK = 2048
ROW_CHUNK = 256
SEG_PITCH = ROW_CHUNK + 8
TM = 512
TQ_MAX = 512
LOG2_E = 1.4426950408889634
TM_E = 1024
FF_CHUNK_E = 512
VMEM_LIMIT = 52 * 1024 * 1024


def _cparams(n_axes, vmem=VMEM_LIMIT):
    return pltpu.CompilerParams(dimension_semantics=("arbitrary",) * n_axes, vmem_limit_bytes=vmem)


def _dot(a, b):
    return jnp.dot(a, b, preferred_element_type=F32)


def _sigmoid(x):
    return 1.0 / (1.0 + jnp.exp(-x))


def _silu(x):
    return x * _sigmoid(x)


def _softplus(x):
    return jnp.maximum(x, 0.0) + jnp.log(1.0 + jnp.exp(-jnp.abs(x)))


def _gelu_tanh(x):
    return 0.5 * x * (1.0 + jnp.tanh(math.sqrt(2.0 / math.pi) * (x + 0.044715 * (x * x * x))))


def _modulated_norm(x, gain, scale, shift):
    ms = jnp.mean(x * x, axis=-1, keepdims=True)
    return (x * lax.rsqrt(ms + EPS)) * (gain * (1.0 + scale)) + shift


def _const_spec(shape):
    zeros = (0,) * len(shape)
    return pl.BlockSpec(shape, lambda *_: zeros, pipeline_mode=pl.Buffered(1))


def _mod_kernel(c_ref, w_ref, b_ref, o_ref):
    c = c_ref[...]
    o_ref[0] = _dot(_silu(c).astype(BF16), w_ref[0].astype(BF16)) + b_ref[0]


def _modulation(cond, mod_w, mod_b):
    depth = mod_w.shape[0]
    n = 6 * D_MODEL
    tn = n // 4
    out = pl.pallas_call(
        _mod_kernel,
        out_shape=jax.ShapeDtypeStruct((depth, 16, n), F32),
        grid=(depth, 4),
        in_specs=[pl.BlockSpec((16, D_MODEL), lambda l, j: (0, 0)),
                  pl.BlockSpec((1, D_MODEL, tn), lambda l, j: (l, 0, j)),
                  pl.BlockSpec((1, 1, tn), lambda l, j: (l, 0, j))],
        out_specs=pl.BlockSpec((1, 16, tn), lambda l, j: (l, 0, j)),
        compiler_params=_cparams(2),
        name="modulation",
    )(cond, mod_w, mod_b.reshape(depth, 1, n))
    return out.reshape(depth, 16, 6, D_MODEL)


def _ev_inproj_kernel(x_ref, mod_ref, g_ref, w_ref, wdt_ref, cw_ref, cb_ref, dtb_ref,
                      p_ref, dt_ref, h_scr, acc_scr, *, seq_is_chunk):
    j = pl.program_id(1)
    rows = h_scr.shape[0]
    n_chunks = rows // ROW_CHUNK
    width = w_ref.shape[1]

    @pl.when(j == 0)
    def _():
        gain, scale, shift = g_ref[...], mod_ref[0, 1:2, :], mod_ref[0, 0:1, :]

        def body(r, carry):
            r0 = pl.multiple_of(r * ROW_CHUNK, ROW_CHUNK)
            h = _modulated_norm(x_ref[pl.ds(r0, ROW_CHUNK), :], gain, scale, shift)
            h_scr[pl.ds(r0, ROW_CHUNK), :] = h.astype(BF16)
            return carry

        lax.fori_loop(0, n_chunks, body, 0)
        dt = _softplus(_dot(h_scr[...], wdt_ref[...]) + dtb_ref[...])
        lane = lax.broadcasted_iota(jnp.int32, dt.shape, 1)
        dt_ref[...] = jnp.where(lane < 2 * H_A, dt, 0.0)
        acc_scr[0:8, :] = jnp.zeros((8, width), F32)
        acc_scr[rows + 8:rows + 16, :] = jnp.zeros((8, width), F32)

    acc_scr[8:rows + 8, :] = _dot(h_scr[...], w_ref[...])

    def plain(act):
        def body(r, carry):
            r0 = pl.multiple_of(r * ROW_CHUNK, ROW_CHUNK)
            v = acc_scr[pl.ds(r0 + 8, ROW_CHUNK), :]
            p_ref[pl.ds(r0, ROW_CHUNK), :] = act(v).astype(BF16)
            return carry
        lax.fori_loop(0, n_chunks, body, 0)

    def conv(act):
        win = ROW_CHUNK + 16
        cw = cw_ref[...]
        cb = cb_ref[...]

        def body(r, carry):
            r0 = pl.multiple_of(r * ROW_CHUNK, ROW_CHUNK)
            xw = acc_scr[pl.ds(r0, win), :]
            if seq_is_chunk:
                row = lax.broadcasted_iota(jnp.int32, xw.shape, 0)
                xw = jnp.where((row >= 8) & (row < ROW_CHUNK + 8), xw, 0.0)
            y = cb + cw[1:2, :] * xw[8:ROW_CHUNK + 8]
            for k in (0, 2, 3):
                sh = pltpu.roll(xw, (1 - k) % win, 0)[8:ROW_CHUNK + 8]
                y = y + cw[k:k + 1, :] * sh
            p_ref[pl.ds(r0, ROW_CHUNK), :] = act(y).astype(BF16)
            return carry
        lax.fori_loop(0, n_chunks, body, 0)

    @pl.when(j < 2)
    def _():
        plain(lambda v: v)

    @pl.when((j >= 2) & (j < 4))
    def _():
        conv(lambda v: v)

    @pl.when((j >= 4) & (j < 6))
    def _():
        plain(_gelu_tanh)

    @pl.when(j >= 6)
    def _():
        conv(_silu)


P_COLS = 4608


def _ev_inproj(x2d, mod_l, cond_of_block, gain, w, wdt, cw, cb, dtb, seq_len):
    n = x2d.shape[0]
    nblk = n // ROW_BLOCK
    tn = 512
    kern = functools.partial(_ev_inproj_kernel, seq_is_chunk=(seq_len == ROW_CHUNK))
    return pl.pallas_call(
        kern,
        out_shape=(jax.ShapeDtypeStruct((n, P_COLS), BF16), jax.ShapeDtypeStruct((n, LANES), F32)),
        grid=(nblk, P_COLS // tn),
        in_specs=[pl.BlockSpec((ROW_BLOCK, D_MODEL), lambda i, j: (i, 0)),
                  pl.BlockSpec((1, 6, D_MODEL), lambda i, j: (cond_of_block(i), 0, 0)),
                  pl.BlockSpec((1, D_MODEL), lambda i, j: (0, 0)),
                  pl.BlockSpec((D_MODEL, tn), lambda i, j: (0, j)),
                  pl.BlockSpec((D_MODEL, LANES), lambda i, j: (0, 0)),
                  pl.BlockSpec((CONV_K, tn), lambda i, j: (0, j)),
                  pl.BlockSpec((1, tn), lambda i, j: (0, j)),
                  pl.BlockSpec((1, LANES), lambda i, j: (0, 0))],
        out_specs=(pl.BlockSpec((ROW_BLOCK, tn), lambda i, j: (i, j)),
                   pl.BlockSpec((ROW_BLOCK, LANES), lambda i, j: (i, 0))),
        scratch_shapes=[pltpu.VMEM((ROW_BLOCK, D_MODEL), BF16),
                        pltpu.VMEM((ROW_BLOCK + 16, tn), F32)],
        compiler_params=_cparams(2),
        name="ev_inproj",
    )(x2d, mod_l, gain, w, wdt, cw, cb, dtb)


def _ssd_direction(d, xs_ref, bm_ref, cm_ref, dt_ref, arate, e2, h_scr, y_ref):
    xs = xs_ref[...].astype(F32)
    bm = bm_ref[...]
    cm = cm_ref[...]
    a = dt_ref[...] * arate
    li = lax.broadcasted_iota(jnp.int32, (CHUNK, CHUNK), 0)
    si = lax.broadcasted_iota(jnp.int32, (CHUNK, CHUNK), 1)
    keep = (si <= li) if d == 0 else (si >= li)
    tri = jnp.where(keep, 1.0, 0.0).astype(F32)
    acs = jnp.dot(tri, a, precision=HIGHEST, preferred_element_type=F32)
    acs_t = acs.T
    tot = acs[CHUNK - 1:CHUNK, :] if d == 0 else acs[0:1, :]
    stack = jnp.concatenate([dt_ref[...], jnp.exp(acs), jnp.exp(tot - acs)], axis=0)
    hi = stack.astype(BF16)
    lo = (stack - hi.astype(F32)).astype(BF16)
    ex = _dot(jnp.concatenate([hi, lo], axis=1), e2)
    dt_e, eacs_e, ds_e = ex[0:CHUNK], ex[CHUNK:2 * CHUNK], ex[2 * CHUNK:3 * CHUNK]
    xdt = xs * dt_e
    xdt_b = xdt.astype(BF16)
    xds_b = (xdt * ds_e).astype(BF16)
    lane = lax.broadcasted_iota(jnp.int32, (CHUNK, LANES), 1)
    gw = (H_A // G_A) * P_A
    y_groups = []
    for g in range(G_A):
        bm_g = bm[:, g * N_A:(g + 1) * N_A]
        cm_g = cm[:, g * N_A:(g + 1) * N_A]
        cb = lax.dot_general(cm_g, bm_g, (((1,), (1,)), ((), ())), preferred_element_type=F32)
        h_prev = h_scr[d, :, g * gw:(g + 1) * gw]
        y_off = _dot(cm_g, h_prev.astype(BF16)) * eacs_e[:, g * gw:(g + 1) * gw]
        bm_t = bm_g.astype(F32).T.astype(BF16)
        st = _dot(bm_t, xds_b[:, g * gw:(g + 1) * gw])
        decay_tot = eacs_e[CHUNK - 1:CHUNK, g * gw:(g + 1) * gw] if d == 0 else eacs_e[0:1, g * gw:(g + 1) * gw]
        h_scr[d, :, g * gw:(g + 1) * gw] = decay_tot * h_prev + st
        pairs = []
        for q in range(H_A // G_A // 2):
            ms = []
            for e in (g * 8 + 2 * q, g * 8 + 2 * q + 1):
                col = acs[:, d * H_A + e:d * H_A + e + 1]
                row = acs_t[d * H_A + e:d * H_A + e + 1, :]
                dec = jnp.where(keep, jnp.exp(jnp.minimum(col - row, 0.0)), 0.0)
                ms.append((cb * dec).astype(BF16))
            lhs = jnp.concatenate(ms, axis=1)
            xp = xdt_b[:, (g * 8 + 2 * q) * P_A:(g * 8 + 2 * q + 2) * P_A]
            zero = jnp.zeros_like(xp)
            rhs = jnp.concatenate([jnp.where(lane < P_A, xp, zero), jnp.where(lane >= P_A, xp, zero)], axis=0)
            pairs.append(_dot(lhs, rhs))
        y_groups.append(jnp.concatenate(pairs, axis=1) + y_off)
    y_ref[...] = jnp.concatenate(y_groups, axis=1)


def _ssd_kernel(*refs, has_h0, emit_final):
    (xsf, bmf, cmf, dtf, xsb, bmb, cmb, dtb, alog_ref, e2_ref), rest = refs[:10], refs[10:]
    if has_h0:
        h0_ref, rest = rest[0], rest[1:]
    yf_ref, yb_ref, rest = rest[0], rest[1], rest[2:]
    if emit_final:
        hfin_ref, rest = rest[0], rest[1:]
    h_scr = rest[0]
    i = pl.program_id(1)

    @pl.when(i == 0)
    def _():
        for d in range(2):
            if has_h0:
                h_scr[d] = h0_ref[0, d].T
            else:
                h_scr[d] = jnp.zeros(h_scr.shape[1:], F32)

    arate = -jnp.exp(alog_ref[...])
    _ssd_direction(0, xsf, bmf, cmf, dtf, arate, e2_ref[0], h_scr, yf_ref)
    _ssd_direction(1, xsb, bmb, cmb, dtb, arate, e2_ref[1], h_scr, yb_ref)

    if emit_final:
        @pl.when(i == pl.num_programs(1) - 1)
        def _():
            for d in range(2):
                hfin_ref[0, d] = h_scr[d].T


def _ssd(p, dt, alog_row, e2, h0, n_batch, seq_len, emit_final):
    n = p.shape[0]
    nc = seq_len // CHUNK
    hp = H_A * P_A
    fwd = lambda b, i: b * nc + i
    bwd = lambda b, i: b * nc + (nc - 1 - i)
    xs_blk, bm_blk, cm_blk = 3, 16, 17

    def specs(rowf):
        return [pl.BlockSpec((CHUNK, hp), lambda b, i: (rowf(b, i), xs_blk)),
                pl.BlockSpec((CHUNK, G_A * N_A), lambda b, i: (rowf(b, i), bm_blk)),
                pl.BlockSpec((CHUNK, G_A * N_A), lambda b, i: (rowf(b, i), cm_blk)),
                pl.BlockSpec((CHUNK, LANES), lambda b, i: (rowf(b, i), 0))]

    in_specs = specs(fwd) + specs(bwd) + [pl.BlockSpec((1, LANES), lambda b, i: (0, 0)),
                                          pl.BlockSpec((2, 2 * LANES, hp), lambda b, i: (0, 0, 0))]
    args = [p, p, p, dt, p, p, p, dt, alog_row, e2]
    if h0 is not None:
        in_specs.append(pl.BlockSpec((1, 2, hp, N_A), lambda b, i: (b, 0, 0, 0)))
        args.append(h0)
    out_shape = [jax.ShapeDtypeStruct((n, hp), F32), jax.ShapeDtypeStruct((n, hp), F32)]
    out_specs = [pl.BlockSpec((CHUNK, hp), lambda b, i: (fwd(b, i), 0)),
                 pl.BlockSpec((CHUNK, hp), lambda b, i: (bwd(b, i), 0))]
    if emit_final:
        out_shape.append(jax.ShapeDtypeStruct((n_batch, 2, hp, N_A), F32))
        out_specs.append(pl.BlockSpec((1, 2, hp, N_A), lambda b, i: (b, 0, 0, 0)))
    kern = functools.partial(_ssd_kernel, has_h0=h0 is not None, emit_final=emit_final)
    return pl.pallas_call(
        kern, out_shape=tuple(out_shape), grid=(n_batch, nc),
        in_specs=in_specs, out_specs=tuple(out_specs),
        scratch_shapes=[pltpu.VMEM((2, N_A, hp), F32)],
        compiler_params=_cparams(2),
        name="ssd",
    )(*args)


def _lru_kernel(*refs, chain, emit_final):
    (xr_ref, gg_ref, wl_ref, lb_ref, lam_ref), rest = refs[:5], refs[5:]
    if chain:
        h0_ref, rest = rest[0], rest[1:]
    y_ref, rest = rest[0], rest[1:]
    if emit_final:
        fin_ref, rest = rest[0], rest[1:]
    a_scr, u_scr, h_scr, p_scr = rest
    rows = xr_ref.shape[0]
    n_seg = rows // ROW_CHUNK
    sp = _softplus(-lam_ref[0])

    def gates(r, carry):
        r0 = pl.multiple_of(r * ROW_CHUNK, ROW_CHUNK)
        s0 = pl.multiple_of(r * SEG_PITCH, 8)
        xr_b = xr_ref[pl.ds(r0, ROW_CHUNK), :]
        xr = xr_b.astype(F32)
        g = _dot(xr_b, wl_ref[0]) + lb_ref[0]
        for d in range(2):
            r_gate = _sigmoid(g[:, (2 * d) * BW_B:(2 * d + 1) * BW_B])
            i_gate = _sigmoid(g[:, (2 * d + 1) * BW_B:(2 * d + 2) * BW_B])
            a = jnp.exp((-LRU_C) * r_gate * sp[d:d + 1, :])
            u = jnp.sqrt(jnp.maximum(1.0 - a * a, 0.0)) * (i_gate * xr)
            a_scr[d, pl.ds(s0, ROW_CHUNK), :] = a
            u_scr[d, pl.ds(s0, ROW_CHUNK), :] = u
        return carry

    lax.fori_loop(0, n_seg, gates, 0)

    def step(t, carry):
        hf, hb, pf, pb = carry
        tb = ROW_CHUNK - 1 - t
        sf = pl.ds(t, n_seg, stride=SEG_PITCH)
        sb = pl.ds(tb, n_seg, stride=SEG_PITCH)
        af = a_scr[0, sf, :]
        ab = a_scr[1, sb, :]
        hf = af * hf + u_scr[0, sf, :]
        hb = ab * hb + u_scr[1, sb, :]
        h_scr[0, sf, :] = hf
        h_scr[1, sb, :] = hb
        if chain:
            pf = af * pf
            pb = ab * pb
            p_scr[0, sf, :] = pf
            p_scr[1, sb, :] = pb
        return hf, hb, pf, pb

    z = jnp.zeros((n_seg, BW_B), F32)
    o = jnp.ones((n_seg, BW_B), F32)
    lax.fori_loop(0, ROW_CHUNK, step, (z, z, o, o), unroll=8)

    if emit_final:
        fin_ref[0, 0, 0] = h_scr[0, pl.ds(ROW_CHUNK - 1, n_seg, stride=SEG_PITCH), :]
        fin_ref[0, 0, 1] = h_scr[1, pl.ds(0, n_seg, stride=SEG_PITCH), :]

    if chain:
        carry = h0_ref[0, 0, 0:1, :]
        for s in range(n_seg):
            sl = slice(s * SEG_PITCH, s * SEG_PITCH + ROW_CHUNK)
            h = h_scr[0, sl, :] + p_scr[0, sl, :] * carry
            h_scr[0, sl, :] = h
            carry = h[ROW_CHUNK - 1:ROW_CHUNK, :]
        carry = h0_ref[0, 0, 1:2, :]
        for s in reversed(range(n_seg)):
            sl = slice(s * SEG_PITCH, s * SEG_PITCH + ROW_CHUNK)
            h = h_scr[1, sl, :] + p_scr[1, sl, :] * carry
            h_scr[1, sl, :] = h
            carry = h[0:1, :]

    for s in range(n_seg):
        sl = slice(s * SEG_PITCH, s * SEG_PITCH + ROW_CHUNK)
        gg = gg_ref[s * ROW_CHUNK:(s + 1) * ROW_CHUNK, :].astype(F32)
        y_ref[s * ROW_CHUNK:(s + 1) * ROW_CHUNK, :] = (gg * (h_scr[0, sl, :] + h_scr[1, sl, :])).astype(BF16)


def _lru(p, wl, lb, lam, h0, chain, emit_final):
    n = p.shape[0]
    nblk = n // ROW_BLOCK
    n_seg = ROW_BLOCK // ROW_CHUNK
    xb_blk, gate_blk = D_MODEL // BW_B, 2 * D_MODEL // BW_B
    in_specs = [pl.BlockSpec((ROW_BLOCK, BW_B), lambda i, h: (i, xb_blk + h)),
                pl.BlockSpec((ROW_BLOCK, BW_B), lambda i, h: (i, gate_blk + h)),
                pl.BlockSpec((1, BW_B, 4 * BW_B), lambda i, h: (h, 0, 0)),
                pl.BlockSpec((1, 1, 4 * BW_B), lambda i, h: (h, 0, 0)),
                pl.BlockSpec((1, 2, BW_B), lambda i, h: (h, 0, 0))]
    args = [p, p, wl, lb, lam]
    if chain:
        in_specs.append(pl.BlockSpec((1, 1, 2, BW_B), lambda i, h: (i, h, 0, 0)))
        args.append(h0)
    out_shape = [jax.ShapeDtypeStruct((n, D_MODEL), BF16)]
    out_specs = [pl.BlockSpec((ROW_BLOCK, BW_B), lambda i, h: (i, h))]
    if emit_final:
        out_shape.append(jax.ShapeDtypeStruct((nblk, H_B, 2, n_seg, BW_B), F32))
        out_specs.append(pl.BlockSpec((1, 1, 2, n_seg, BW_B), lambda i, h: (i, h, 0, 0, 0)))
    seg_rows = n_seg * SEG_PITCH
    kern = functools.partial(_lru_kernel, chain=chain, emit_final=emit_final)
    return pl.pallas_call(
        kern, out_shape=tuple(out_shape), grid=(nblk, H_B),
        in_specs=in_specs, out_specs=tuple(out_specs),
        scratch_shapes=[pltpu.VMEM((2, seg_rows, BW_B), F32)] * 4,
        compiler_params=_cparams(2),
        name="lru",
    )(*args)


def _ev_out_kernel(x_ref, yf_ref, yb_ref, xs_ref, z_ref, yl_ref, wo_ref, dsk_ref, gn_ref, mod_ref, o_ref):
    y = yf_ref[...] + yb_ref[...] + dsk_ref[...] * xs_ref[...].astype(F32)
    y = y * _silu(z_ref[...].astype(F32))
    gw = D_MODEL // G_A
    parts = []
    for g in range(G_A):
        yg = y[:, g * gw:(g + 1) * gw]
        ms = jnp.mean(yg * yg, axis=-1, keepdims=True)
        parts.append((yg * lax.rsqrt(ms + EPS) * gn_ref[:, g * gw:(g + 1) * gw]).astype(BF16))
    mix = _dot(parts[0], wo_ref[0:gw, :]) + _dot(parts[1], wo_ref[gw:2 * gw, :])
    mix = mix + _dot(yl_ref[...], wo_ref[D_MODEL:2 * D_MODEL, :])
    o_ref[...] = x_ref[...] + mod_ref[0, 2:3, :] * mix


def _ev_out(x2d, yf, yb, p, ylru, wo, dsk, gn, mod_l, cond_of_tile):
    n = x2d.shape[0]
    row = lambda i: (i, 0)
    return pl.pallas_call(
        _ev_out_kernel,
        out_shape=jax.ShapeDtypeStruct((n, D_MODEL), F32),
        grid=(n // TM,),
        in_specs=[pl.BlockSpec((TM, D_MODEL), row), pl.BlockSpec((TM, D_MODEL), row),
                  pl.BlockSpec((TM, D_MODEL), row),
                  pl.BlockSpec((TM, D_MODEL), lambda i: (i, 3)),
                  pl.BlockSpec((TM, D_MODEL), lambda i: (i, 0)),
                  pl.BlockSpec((TM, D_MODEL), row),
                  _const_spec((2 * D_MODEL, D_MODEL)), _const_spec((1, D_MODEL)), _const_spec((1, D_MODEL)),
                  pl.BlockSpec((1, 6, D_MODEL), lambda i: (cond_of_tile(i), 0, 0))],
        out_specs=pl.BlockSpec((TM, D_MODEL), row),
        compiler_params=_cparams(1),
        name="ev_out",
    )(x2d, yf, yb, p, p, ylru, wo, dsk, gn, mod_l)


def _ffn_kernel(x_ref, mod_ref, g_ref, wgu_ref, wd_ref, o_ref):
    x = x_ref[...]
    h = _modulated_norm(x, g_ref[...], mod_ref[0, 4:5, :], mod_ref[0, 3:4, :]).astype(BF16)
    n_chunk = 2
    fc = D_FF // n_chunk
    f = None
    for c in range(n_chunk):
        g = _dot(h, wgu_ref[:, c * fc:(c + 1) * fc])
        u = _dot(h, wgu_ref[:, D_FF + c * fc:D_FF + (c + 1) * fc])
        part = _dot((_silu(g) * u).astype(BF16), wd_ref[c * fc:(c + 1) * fc, :])
        f = part if f is None else f + part
    o_ref[...] = x + mod_ref[0, 5:6, :] * f


def _ffn(x2d, mod_l, cond_of_tile, gain, wgu, wd):
    n = x2d.shape[0]
    return pl.pallas_call(
        _ffn_kernel,
        out_shape=jax.ShapeDtypeStruct((n, D_MODEL), F32),
        grid=(n // TM,),
        in_specs=[pl.BlockSpec((TM, D_MODEL), lambda i: (i, 0)),
                  pl.BlockSpec((1, 6, D_MODEL), lambda i: (cond_of_tile(i), 0, 0)),
                  _const_spec((1, D_MODEL)), _const_spec((D_MODEL, 2 * D_FF)), _const_spec((D_FF, D_MODEL))],
        out_specs=pl.BlockSpec((TM, D_MODEL), lambda i: (i, 0)),
        compiler_params=_cparams(1),
        name="ffn",
    )(x2d, mod_l, gain, wgu, wd)


OD_IN_COLS = 896


def _od_inproj_kernel(*refs, use_rope):
    (x_ref, mod_ref, g_ref, wi_ref, qn_ref, kvn_ref, wq_ref), rest = refs[:7], refs[7:]
    if use_rope:
        (wqs_ref, cq_ref, sq_ref, ck_ref, sk_ref), rest = rest[:5], rest[5:]
    q_ref, ckv_ref, kr_ref = rest
    h = _modulated_norm(x_ref[...], g_ref[...], mod_ref[0, 1:2, :], mod_ref[0, 0:1, :]).astype(BF16)
    proj = _dot(h, wi_ref[...])
    cq = proj[:, 0:Q_RANK]
    cqn = (cq * lax.rsqrt(jnp.mean(cq * cq, axis=-1, keepdims=True) + EPS) * qn_ref[...]).astype(BF16)
    ckv = proj[:, Q_RANK:Q_RANK + KV_RANK]
    ckv_ref[...] = ckv * lax.rsqrt(jnp.mean(ckv * ckv, axis=-1, keepdims=True) + EPS) * kvn_ref[...]
    kr = proj[:, 640:640 + ROPE_C]
    q = _dot(cqn, wq_ref[...])
    if use_rope:
        q = q * jnp.tile(cq_ref[...], (1, H_C)) + _dot(cqn, wqs_ref[...]) * jnp.tile(sq_ref[...], (1, H_C))
        kr = kr * ck_ref[...] + proj[:, 768:768 + ROPE_C] * sk_ref[...]
    q = (q * ((NOPE_C + ROPE_C) ** -0.5 * LOG2_E)).astype(BF16)
    for h in range(H_C):
        q_ref[h] = q[:, h * HEAD_PAD:(h + 1) * HEAD_PAD]
    kr_ref[...] = kr


def _od_inproj(x2d, mod_l, cond_of_tile, gain, wi, qn, kvn, wq, rope, seq_len):
    n = x2d.shape[0]
    use_rope = rope is not None
    in_specs = [pl.BlockSpec((TM, D_MODEL), lambda i: (i, 0)),
                pl.BlockSpec((1, 6, D_MODEL), lambda i: (cond_of_tile(i), 0, 0)),
                _const_spec((1, D_MODEL)), _const_spec((D_MODEL, OD_IN_COLS)),
                _const_spec((1, Q_RANK)), _const_spec((1, KV_RANK)),
                _const_spec((Q_RANK, H_C * HEAD_PAD))]
    args = [x2d, mod_l, gain, wi, qn, kvn, wq]
    if use_rope:
        wqs, cos_q, sin_q, cos_k, sin_k = rope
        per_seq = seq_len // TM
        pos = lambda i: (i % per_seq, 0)
        in_specs += [_const_spec((Q_RANK, H_C * HEAD_PAD)),
                     pl.BlockSpec((TM, HEAD_PAD), pos), pl.BlockSpec((TM, HEAD_PAD), pos),
                     pl.BlockSpec((TM, ROPE_C), pos), pl.BlockSpec((TM, ROPE_C), pos)]
        args += [wqs, cos_q, sin_q, cos_k, sin_k]
    return pl.pallas_call(
        functools.partial(_od_inproj_kernel, use_rope=use_rope),
        out_shape=(jax.ShapeDtypeStruct((H_C, n, HEAD_PAD), BF16),
                   jax.ShapeDtypeStruct((n, KV_RANK), F32),
                   jax.ShapeDtypeStruct((n, ROPE_C), F32)),
        grid=(n // TM,),
        in_specs=in_specs,
        out_specs=(pl.BlockSpec((H_C, TM, HEAD_PAD), lambda i: (0, i, 0)),
                   pl.BlockSpec((TM, KV_RANK), lambda i: (i, 0)),
                   pl.BlockSpec((TM, ROPE_C), lambda i: (i, 0))),
        compiler_params=_cparams(1),
        name="od_inproj",
    )(*args)


def _kv_up_kernel(ckv_ref, kr_ref, wk_ref, ek_ref, wv_ref, k_ref, v_ref):
    ckv = ckv_ref[...].astype(BF16)
    k = (_dot(ckv, wk_ref[...]) + _dot(kr_ref[...].astype(BF16), ek_ref[...])).astype(BF16)
    v = _dot(ckv, wv_ref[...]).astype(BF16)
    for h in range(H_C):
        k_ref[h] = k[:, h * HEAD_PAD:(h + 1) * HEAD_PAD]
    for hp in range(H_C // 2):
        v_ref[hp] = v[:, hp * LANES:(hp + 1) * LANES]


def _kv_up(ckv_all, kr_all, wk, ek, wv):
    m = ckv_all.shape[0]
    return pl.pallas_call(
        _kv_up_kernel,
        out_shape=(jax.ShapeDtypeStruct((H_C, m, HEAD_PAD), BF16),
                   jax.ShapeDtypeStruct((H_C // 2, m, LANES), BF16)),
        grid=(m // TM,),
        in_specs=[pl.BlockSpec((TM, KV_RANK), lambda i: (i, 0)), pl.BlockSpec((TM, ROPE_C), lambda i: (i, 0)),
                  _const_spec((KV_RANK, H_C * HEAD_PAD)), _const_spec((ROPE_C, H_C * HEAD_PAD)),
                  _const_spec((KV_RANK, H_C * V_C))],
        out_specs=(pl.BlockSpec((H_C, TM, HEAD_PAD), lambda i: (0, i, 0)),
                   pl.BlockSpec((H_C // 2, TM, LANES), lambda i: (0, i, 0))),
        compiler_params=_cparams(1),
        name="kv_up",
    )(ckv_all, kr_all, wk, ek, wv)


def _attn_kernel(q_ref, k_ref, v_ref, o_ref):
    lane = lax.broadcasted_iota(jnp.int32, (q_ref.shape[1], LANES), 1)

    def head_pair(hp, carry):
        outs = []
        for side in range(2):
            h = 2 * hp + side
            s = lax.dot_general(q_ref[h], k_ref[h], (((1,), (1,)), ((), ())), preferred_element_type=F32)
            p = jnp.exp2(s - jnp.max(s, axis=-1, keepdims=True))
            l = jnp.sum(p, axis=-1, keepdims=True)
            outs.append(_dot(p.astype(BF16), v_ref[hp]) / l)
        o_ref[hp] = jnp.where(lane < V_C, outs[0], outs[1]).astype(BF16)
        return carry

    lax.fori_loop(0, H_C // 2, head_pair, 0)


def _attention(q, k, v, n_batch, seq_len, kv_len):
    tq = min(seq_len, TQ_MAX)
    per = seq_len // tq
    return pl.pallas_call(
        _attn_kernel,
        out_shape=jax.ShapeDtypeStruct((H_C // 2, q.shape[1], LANES), BF16),
        grid=(n_batch, per),
        in_specs=[pl.BlockSpec((H_C, tq, HEAD_PAD), lambda b, i: (0, b * per + i, 0)),
                  pl.BlockSpec((H_C, kv_len, HEAD_PAD), lambda b, i: (0, b, 0)),
                  pl.BlockSpec((H_C // 2, kv_len, LANES), lambda b, i: (0, b, 0))],
        out_specs=pl.BlockSpec((H_C // 2, tq, LANES), lambda b, i: (0, b * per + i, 0)),
        compiler_params=_cparams(2),
        name="attention",
    )(q, k, v)


TOKEN_ROWS = D_MODEL // LANES


def _store_token_rows(ref, value, index=()):
    for s in range(TOKEN_ROWS):
        ref[index + (pl.ds(s, value.shape[0], stride=TOKEN_ROWS), slice(None))] = value[:, s * LANES:(s + 1) * LANES]


def _load_token_rows(ref, rows, index=()):
    parts = [ref[index + (pl.ds(s, rows, stride=TOKEN_ROWS), slice(None))] for s in range(TOKEN_ROWS)]
    return jnp.concatenate(parts, axis=-1)


ROUTE_IDX0, ROUTE_IDX1, ROUTE_RANK0, ROUTE_RANK1, ROUTE_W0, ROUTE_W1 = range(6)


def _od_out_router_kernel(oa_ref, ob_ref, xa_ref, xb_ref, wo_ref, mod_ref, g_ref, wr_ref,
                          x3_ref, h2_ref, route_ref, cnt_ref, cnt_scr, *, tiles_a):
    i = pl.program_id(0)

    @pl.when(i == 0)
    def _():
        cnt_scr[...] = jnp.zeros(cnt_scr.shape, F32)

    from_a = i < tiles_a
    o = jnp.concatenate([jnp.where(from_a, oa_ref[hp], ob_ref[hp]) for hp in range(H_C // 2)], axis=-1)
    x3 = jnp.where(from_a, xa_ref[...], xb_ref[...]) + mod_ref[0, 2:3, :] * _dot(o, wo_ref[...])
    x3_ref[...] = x3
    h2 = _modulated_norm(x3, g_ref[...], mod_ref[0, 4:5, :], mod_ref[0, 3:4, :])
    _store_token_rows(h2_ref, h2)
    h_hi = h2.astype(BF16)
    h_lo = (h2 - h_hi.astype(F32)).astype(BF16)
    w_hi, w_lo = wr_ref[0], wr_ref[1]
    logits = _dot(h_hi, w_hi) + (_dot(h_lo, w_hi) + _dot(h_hi, w_lo))
    tm = logits.shape[0]
    lane = lax.broadcasted_iota(jnp.int32, logits.shape, 1)
    neg = jnp.float32(-jnp.inf)
    logits = jnp.where(lane < N_EXPERTS, logits, neg)
    m0 = jnp.max(logits, axis=-1, keepdims=True)
    lane_f = lane.astype(F32)
    i0 = jnp.min(jnp.where(logits == m0, lane_f, float(LANES)), axis=-1, keepdims=True)
    rest = jnp.where(lane_f == i0, neg, logits)
    m1 = jnp.max(rest, axis=-1, keepdims=True)
    i1 = jnp.min(jnp.where(rest == m1, lane_f, float(LANES)), axis=-1, keepdims=True)
    e = jnp.exp(m1 - m0)
    w0 = 1.0 / (1.0 + e)
    w1 = e * w0
    oh0 = jnp.where(lane_f == i0, 1.0, 0.0).astype(F32)
    oh1 = jnp.where(lane_f == i1, 1.0, 0.0).astype(F32)
    sel = oh0 + oh1
    ri = lax.broadcasted_iota(jnp.int32, (tm, tm), 0)
    ci = lax.broadcasted_iota(jnp.int32, (tm, tm), 1)
    before = jnp.where(ci < ri, 1.0, 0.0).astype(BF16)
    prior = _dot(before, sel.astype(BF16)) + cnt_scr[0:1, :]
    r0 = jnp.sum(oh0 * prior, axis=-1, keepdims=True)
    r1 = jnp.sum(oh1 * prior, axis=-1, keepdims=True)
    route = jnp.where(lane == ROUTE_IDX0, i0, 0.0)
    route = jnp.where(lane == ROUTE_IDX1, i1, route)
    route = jnp.where(lane == ROUTE_RANK0, r0, route)
    route = jnp.where(lane == ROUTE_RANK1, r1, route)
    route = jnp.where(lane == ROUTE_W0, w0, route)
    route = jnp.where(lane == ROUTE_W1, w1, route)
    route_ref[...] = route
    total = cnt_scr[0:1, :] + jnp.sum(sel, axis=0, keepdims=True)
    cnt_scr[...] = jnp.broadcast_to(total, cnt_scr.shape)
    cnt_ref[...] = cnt_scr[...]


def _od_out_router(o_a, o_b, x_a, x_b, wo, mod_l, cond_of_tile, gain, wr):
    n_a, n_b = x_a.shape[0], x_b.shape[0]
    n = n_a + n_b
    tiles_a = n_a // TM
    row = lambda i: (i, 0)
    a_tile = lambda i: jnp.minimum(i, tiles_a - 1)
    b_tile = lambda i: jnp.maximum(i - tiles_a, 0)
    return pl.pallas_call(
        functools.partial(_od_out_router_kernel, tiles_a=tiles_a),
        out_shape=(jax.ShapeDtypeStruct((n, D_MODEL), F32), jax.ShapeDtypeStruct((n * TOKEN_ROWS, LANES), F32),
                   jax.ShapeDtypeStruct((n, LANES), F32), jax.ShapeDtypeStruct((8, LANES), F32)),
        grid=(n // TM,),
        in_specs=[pl.BlockSpec((H_C // 2, TM, LANES), lambda i: (0, a_tile(i), 0)),
                  pl.BlockSpec((H_C // 2, TM, LANES), lambda i: (0, b_tile(i), 0)),
                  pl.BlockSpec((TM, D_MODEL), lambda i: (a_tile(i), 0)),
                  pl.BlockSpec((TM, D_MODEL), lambda i: (b_tile(i), 0)),
                  _const_spec((D_MODEL, D_MODEL)),
                  pl.BlockSpec((1, 6, D_MODEL), lambda i: (cond_of_tile(i), 0, 0)),
                  _const_spec((1, D_MODEL)), _const_spec((2, D_MODEL, LANES))],
        out_specs=(pl.BlockSpec((TM, D_MODEL), row), pl.BlockSpec((TM * TOKEN_ROWS, LANES), row),
                   pl.BlockSpec((TM, LANES), row), pl.BlockSpec((8, LANES), lambda i: (0, 0))),
        scratch_shapes=[pltpu.VMEM((8, LANES), F32)],
        compiler_params=_cparams(1),
        name="od_out_router",
    )(o_a, o_b, x_a, x_b, wo, mod_l, gain, wr)


def _token_copy(src, s8, dst, d8, sem):
    return pltpu.make_async_copy(src.at[pl.ds(pl.multiple_of(s8, TOKEN_ROWS), TOKEN_ROWS)],
                                 dst.at[pl.ds(pl.multiple_of(d8, TOKEN_ROWS), TOKEN_ROWS)], sem)


ROW_DMA_UNROLL = 8


def _experts_kernel(te_ref, nt_ref, src0_ref, src1_ref, dst_ref, h_hbm, wg_ref, wu_ref, wd_ref, y_hbm,
                    h_buf, y_buf, h_scr, acc_scr, sem_in, sem_out):
    i, c = pl.program_id(0), pl.program_id(1)
    last_c = pl.num_programs(1) - 1
    n_tiles = nt_ref[0]
    tile_rows = TM_E * TOKEN_ROWS

    def gather(buf, src_ref):
        def body(r, carry):
            _token_copy(h_hbm, src_ref[r], h_buf.at[buf], r * TOKEN_ROWS, sem_in.at[buf]).start()
            return carry
        lax.fori_loop(0, TM_E, body, 0, unroll=ROW_DMA_UNROLL)

    def tile_copy(src, dst, sem):
        return pltpu.make_async_copy(src.at[pl.ds(0, tile_rows)], dst.at[pl.ds(0, tile_rows)], sem)

    @pl.when(i < n_tiles)
    def _():
        @pl.when(c == 0)
        def _():
            @pl.when(i == 0)
            def _():
                gather(0, src0_ref)
                y_buf[...] = jnp.zeros(y_buf.shape, F32)
                spare = pltpu.make_async_copy(y_buf, y_hbm.at[pl.ds(y_hbm.shape[0] - tile_rows, tile_rows)], sem_out)
                spare.start()
                spare.wait()

            @pl.when(i + 1 < n_tiles)
            def _():
                gather((i + 1) % 2, src1_ref)

            buf = i % 2
            tile_copy(h_hbm, h_buf.at[buf], sem_in.at[buf]).wait()
            h_scr[...] = _load_token_rows(h_buf, TM_E, (buf,)).astype(BF16)

        h = h_scr[...]
        g = _dot(h, wg_ref[0].astype(BF16))
        u = _dot(h, wu_ref[0].astype(BF16))
        part = _dot((_silu(g) * u).astype(BF16), wd_ref[0].astype(BF16))

        @pl.when(c == 0)
        def _():
            acc_scr[...] = part

        @pl.when((c > 0) & (c < last_c))
        def _():
            acc_scr[...] += part

        @pl.when(c == last_c)
        def _():
            @pl.when(i > 0)
            def _():
                tile_copy(y_buf, y_hbm, sem_out).wait()
            _store_token_rows(y_buf, acc_scr[...] + part)

            def body(r, carry):
                _token_copy(y_buf, r * TOKEN_ROWS, y_hbm, dst_ref[r], sem_out).start()
                return carry
            lax.fori_loop(0, TM_E, body, 0, unroll=ROW_DMA_UNROLL)

            @pl.when(i == n_tiles - 1)
            def _():
                tile_copy(y_buf, y_hbm, sem_out).wait()


def _experts(tile_expert, n_tiles, slot_src, slot_dst, h_all, wgu, wd, n_out_tokens):
    rows = slot_src.shape[0]
    t = rows // TM_E
    n_c = D_FF_E // FF_CHUNK_E
    assert n_c > 1
    live = lambda i, nt: jnp.maximum(jnp.minimum(i, nt[0] - 1), 0)
    smem_rows = lambda f: pl.BlockSpec((TM_E,), f, memory_space=pltpu.SMEM)
    grid_spec = pltpu.PrefetchScalarGridSpec(
        num_scalar_prefetch=2,
        grid=(t, n_c),
        in_specs=[smem_rows(lambda i, c, te, nt: (0,)),
                  smem_rows(lambda i, c, te, nt: (jnp.minimum(i + 1, t - 1),)),
                  smem_rows(lambda i, c, te, nt: (i,)),
                  pl.BlockSpec(memory_space=pl.ANY),
                  pl.BlockSpec((1, D_MODEL, FF_CHUNK_E), lambda i, c, te, nt: (te[live(i, nt)], 0, c)),
                  pl.BlockSpec((1, D_MODEL, FF_CHUNK_E), lambda i, c, te, nt: (te[live(i, nt)], 0, n_c + c)),
                  pl.BlockSpec((1, FF_CHUNK_E, D_MODEL), lambda i, c, te, nt: (te[live(i, nt)], c, 0))],
        out_specs=pl.BlockSpec(memory_space=pl.ANY),
        scratch_shapes=[pltpu.VMEM((2, TM_E * TOKEN_ROWS, LANES), F32), pltpu.VMEM((TM_E * TOKEN_ROWS, LANES), F32),
                        pltpu.VMEM((TM_E, D_MODEL), BF16), pltpu.VMEM((TM_E, D_MODEL), F32),
                        pltpu.SemaphoreType.DMA((2,)), pltpu.SemaphoreType.DMA(())],
    )
    return pl.pallas_call(
        _experts_kernel,
        out_shape=jax.ShapeDtypeStruct(((n_out_tokens + TM_E) * TOKEN_ROWS, LANES), F32),
        grid_spec=grid_spec,
        compiler_params=pltpu.CompilerParams(dimension_semantics=("arbitrary", "arbitrary"),
                                             vmem_limit_bytes=VMEM_LIMIT, has_side_effects=True),
        name="experts",
    )(tile_expert, n_tiles, slot_src, slot_src, slot_dst, h_all, wgu, wgu, wd)


def _combine_kernel(x_ref, ya_ref, yb_ref, route_ref, mod_ref, g_ref, o_ref):
    w0 = route_ref[:, ROUTE_W0:ROUTE_W0 + 1]
    w1 = route_ref[:, ROUTE_W1:ROUTE_W1 + 1]
    rows = x_ref.shape[0]
    x = x_ref[...] + mod_ref[0, 5:6, :] * (w0 * _load_token_rows(ya_ref, rows) + w1 * _load_token_rows(yb_ref, rows))
    o_ref[...] = x * lax.rsqrt(jnp.mean(x * x, axis=-1, keepdims=True) + EPS) * g_ref[...]


def _combine(x3, y2, route, first_row, n_rows, n_tok, mod_l, cond_of_tile, final_g):
    nt = n_rows // TM
    blk0 = first_row // TM
    blk1 = (n_tok + first_row) // TM
    return pl.pallas_call(
        _combine_kernel,
        out_shape=jax.ShapeDtypeStruct((n_rows, D_MODEL), F32),
        grid=(nt,),
        in_specs=[pl.BlockSpec((TM, D_MODEL), lambda i: (blk0 + i, 0)),
                  pl.BlockSpec((TM * TOKEN_ROWS, LANES), lambda i: (blk0 + i, 0)),
                  pl.BlockSpec((TM * TOKEN_ROWS, LANES), lambda i: (blk1 + i, 0)),
                  pl.BlockSpec((TM, LANES), lambda i: (blk0 + i, 0)),
                  pl.BlockSpec((1, 6, D_MODEL), lambda i: (cond_of_tile(blk0 + i), 0, 0)),
                  _const_spec((1, D_MODEL))],
        out_specs=pl.BlockSpec((TM, D_MODEL), lambda i: (i, 0)),
        compiler_params=_cparams(1),
        name="combine",
    )(x3, y2, y2, route, mod_l, final_g)


def _even_params(ev_w_in, ev_conv_a_w, ev_conv_a_b, ev_a_log, ev_dt_bias, ev_d_skip, ev_gnorm,
                 ev_conv_b_w, ev_conv_b_b, ev_lru_w, ev_lru_b, ev_lru_lam, ev_w_out, ev_ffn_gu, ev_ffn_down):
    d_a = H_A * P_A
    xbc = d_a + 2 * G_A * N_A
    o_xbc, o_dt, o_gate, o_xb = d_a, d_a + xbc, d_a + xbc + 2 * H_A, d_a + xbc + 2 * H_A + D_MODEL
    w = ev_w_in
    w_main = jnp.concatenate([w[:, :d_a], w[:, o_xb:], w[:, o_gate:o_xb], w[:, o_xbc:o_dt]], axis=1).astype(BF16)
    w_dt = jnp.pad(w[:, o_dt:o_gate], ((0, 0), (0, LANES - 2 * H_A))).astype(BF16)
    zeros_w = jnp.zeros((CONV_K, D_MODEL), F32)
    cw = jnp.concatenate([zeros_w, ev_conv_b_w, zeros_w, ev_conv_a_w], axis=1)
    zeros_b = jnp.zeros((D_MODEL,), F32)
    cb = jnp.concatenate([zeros_b, ev_conv_b_b, zeros_b, ev_conv_a_b])[None, :]
    dtb = jnp.pad(ev_dt_bias.reshape(-1), (0, LANES - 2 * H_A))[None, :]
    alog = jnp.pad(ev_a_log.reshape(-1), (0, LANES - 2 * H_A))[None, :]
    j = jnp.arange(2 * LANES) % LANES
    c = jnp.arange(d_a) // P_A
    e2 = jnp.stack([(j[:, None] == (d * H_A + c)[None, :]) for d in range(2)]).astype(BF16)
    wl = jnp.transpose(ev_lru_w, (2, 3, 0, 1, 4)).reshape(H_B, BW_B, 4 * BW_B).astype(BF16)
    lb = jnp.transpose(ev_lru_b.reshape(2, 2, H_B, BW_B), (2, 0, 1, 3)).reshape(H_B, 1, 4 * BW_B)
    lam = jnp.transpose(ev_lru_lam.reshape(2, H_B, BW_B), (1, 0, 2))
    dsk = jnp.repeat(ev_d_skip, P_A)[None, :]
    return dict(w_main=w_main, w_dt=w_dt, cw=cw, cb=cb, dtb=dtb, alog=alog, e2=e2, wl=wl, lb=lb, lam=lam,
                dsk=dsk, gn=ev_gnorm[None, :], w_out=ev_w_out.astype(BF16),
                wgu=ev_ffn_gu.astype(BF16), wd=ev_ffn_down.astype(BF16))


def _rope_partner(w, lo):
    blk = w[:, lo:lo + ROPE_C].reshape(w.shape[0], 2, 2, N_FREQ)
    return jnp.flip(blk, axis=2).reshape(w.shape[0], ROPE_C)


def _odd_params(od_w_in, od_q_norm, od_w_q_up, od_kv_norm, od_w_kv_up, od_w_o, od_router):
    kr_lo = Q_RANK + KV_RANK
    wi = jnp.zeros((D_MODEL, OD_IN_COLS), F32)
    wi = wi.at[:, :kr_lo + ROPE_C].set(od_w_in)
    wi = wi.at[:, 768:768 + ROPE_C].set(_rope_partner(od_w_in, kr_lo))
    wq3 = od_w_q_up.reshape(Q_RANK, H_C, NOPE_C + ROPE_C)
    pad = HEAD_PAD - NOPE_C - ROPE_C
    wq = jnp.pad(wq3, ((0, 0), (0, 0), (0, pad))).reshape(Q_RANK, H_C * HEAD_PAD)
    partner = jnp.flip(wq3[:, :, NOPE_C:].reshape(Q_RANK, H_C, 2, 2, N_FREQ), axis=3).reshape(Q_RANK, H_C, ROPE_C)
    wqs = jnp.pad(partner, ((0, 0), (0, 0), (NOPE_C, pad))).reshape(Q_RANK, H_C * HEAD_PAD)
    wkv3 = od_w_kv_up.reshape(KV_RANK, H_C, NOPE_C + V_C)
    wk = jnp.pad(wkv3[:, :, :NOPE_C], ((0, 0), (0, 0), (0, HEAD_PAD - NOPE_C))).reshape(KV_RANK, H_C * HEAD_PAD)
    wv = wkv3[:, :, NOPE_C:].reshape(KV_RANK, H_C * V_C)
    ek_head = jnp.pad(jnp.eye(ROPE_C, dtype=F32), ((0, 0), (NOPE_C, pad)))
    ek = jnp.tile(ek_head, (1, H_C))
    wr_f = jnp.pad(od_router, ((0, 0), (0, LANES - N_EXPERTS)))
    wr_hi = wr_f.astype(BF16)
    wr = jnp.stack([wr_hi, (wr_f - wr_hi.astype(F32)).astype(BF16)])
    return dict(wi=wi.astype(BF16), qn=od_q_norm[None, :], kvn=od_kv_norm[None, :], wq=wq.astype(BF16),
                wqs=wqs.astype(BF16), wk=wk.astype(BF16), ek=ek.astype(BF16), wv=wv.astype(BF16),
                wo=od_w_o.astype(BF16), wr=wr)


def _rope_tables(n_tokens):
    rows = n_tokens // GRID_W
    row = jnp.repeat(jnp.arange(rows), GRID_W).astype(F32)
    col = jnp.tile(jnp.arange(GRID_W), rows).astype(F32)
    inv = ROPE_BASE ** (-jnp.arange(N_FREQ, dtype=F32) / N_FREQ)
    ang_r, ang_c = row[:, None] * inv, col[:, None] * inv
    cos_k = jnp.concatenate([jnp.cos(ang_r)] * 2 + [jnp.cos(ang_c)] * 2, axis=1)
    sin_k = jnp.concatenate([-jnp.sin(ang_r), jnp.sin(ang_r), -jnp.sin(ang_c), jnp.sin(ang_c)], axis=1)
    pad = HEAD_PAD - NOPE_C - ROPE_C
    cos_q = jnp.pad(cos_k, ((0, 0), (NOPE_C, pad)), constant_values=1.0)
    sin_q = jnp.pad(sin_k, ((0, 0), (NOPE_C, pad)))
    return cos_q, sin_q, cos_k, sin_k


def _layer0(x2d, mod_l, cond_row, norm_g0, ev, n_batch, seq_len, h0_ssd, h0_lru, is_ctx):
    per_block = max(ROW_BLOCK // seq_len, 1)
    blocks_per_seq = max(seq_len // ROW_BLOCK, 1)
    cond_of_block = (lambda i: 0) if is_ctx else (lambda i: cond_row(i // blocks_per_seq))
    tiles_per_seq = seq_len // TM if seq_len >= TM else 1
    cond_of_tile = (lambda i: 0) if is_ctx else (lambda i: cond_row(i // tiles_per_seq))
    p, dt = _ev_inproj(x2d, mod_l, cond_of_block, norm_g0[0:1], ev["w_main"], ev["w_dt"], ev["cw"], ev["cb"],
                       ev["dtb"], seq_len)
    ssd_out = _ssd(p, dt, ev["alog"], ev["e2"], h0_ssd, n_batch, seq_len, emit_final=is_ctx)
    lru_out = _lru(p, ev["wl"], ev["lb"], ev["lam"], h0_lru, chain=not is_ctx, emit_final=is_ctx)
    x1 = _ev_out(x2d, ssd_out[0], ssd_out[1], p, lru_out[0], ev["w_out"], ev["dsk"], ev["gn"], mod_l, cond_of_tile)
    x2 = _ffn(x1, mod_l, cond_of_tile, norm_g0[1:2], ev["wgu"], ev["wd"])
    finals = None
    if is_ctx:
        hp = H_A * P_A
        s_ssd = ssd_out[2].reshape(n_batch, 1, 2, H_A, P_A, N_A)
        fin = lru_out[1]
        s_lru = jnp.transpose(fin, (0, 3, 2, 1, 4)).reshape(n_batch, 1, 2, H_B * BW_B)
        finals = (s_ssd, s_lru)
        del hp, per_block
    return x2, finals, cond_of_tile


def _layer1_pre(x2d, mod_l, cond_of_tile, norm_g1, od, n_batch, seq_len, rope, cache):
    q, ckv, kr = _od_inproj(x2d, mod_l, cond_of_tile, norm_g1[0:1], od["wi"], od["qn"], od["kvn"], od["wq"],
                            rope, seq_len)
    if cache is None:
        ckv_all, kr_all, kv_len = ckv, kr, seq_len
    else:
        c_ckv, c_kr = cache
        past = c_ckv.shape[1]
        kv_len = past + seq_len
        ckv_all = jnp.concatenate([c_ckv, ckv.reshape(n_batch, seq_len, KV_RANK)], axis=1).reshape(-1, KV_RANK)
        kr_all = jnp.concatenate([c_kr, kr.reshape(n_batch, seq_len, ROPE_C)], axis=1).reshape(-1, ROPE_C)
    k, v = _kv_up(ckv_all, kr_all, od["wk"], od["ek"], od["wv"])
    o = _attention(q, k, v, n_batch, seq_len, kv_len)
    return o, ckv, kr


def kernel(x_prompt, x_sample, state_ssd, state_lru, cache_ckv, cache_krope, c, c_ctx, mod_w, mod_b, norm_g, final_g, ev_w_in, ev_conv_a_w, ev_conv_a_b, ev_a_log, ev_dt_bias, ev_d_skip, ev_gnorm, ev_conv_b_w, ev_conv_b_b, ev_lru_w, ev_lru_b, ev_lru_lam, ev_w_out, ev_ffn_gu, ev_ffn_down, od_w_in, od_q_norm, od_w_q_up, od_kv_norm, od_w_kv_up, od_w_o, od_router, od_moe_gu, od_moe_down):
    batch, seq, _ = x_prompt.shape
    dbatch, dseq, _ = x_sample.shape
    n_ctx, n_dec = batch * seq, dbatch * dseq

    cond = jnp.concatenate([c_ctx[None, :], c, jnp.zeros((16 - 1 - dbatch, D_MODEL), F32)], axis=0)
    mod = _modulation(cond, mod_w, mod_b)
    ev = _even_params(ev_w_in[0], ev_conv_a_w[0], ev_conv_a_b[0], ev_a_log[0], ev_dt_bias[0], ev_d_skip[0],
                      ev_gnorm[0], ev_conv_b_w[0], ev_conv_b_b[0], ev_lru_w[0], ev_lru_b[0], ev_lru_lam[0],
                      ev_w_out[0], ev_ffn_gu[0], ev_ffn_down[0])
    od = _odd_params(od_w_in[0], od_q_norm[0], od_w_q_up[0], od_kv_norm[0], od_w_kv_up[0], od_w_o[0], od_router[0])
    wgu_e = od_moe_gu[0]
    wd_e = od_moe_down[0]
    dec_row = lambda b: 1 + b

    xc, finals, tile_c = _layer0(x_prompt.reshape(n_ctx, D_MODEL), mod[0], None, norm_g[0], ev, batch, seq,
                                 None, None, True)
    h0_ssd = state_ssd[:, 0].reshape(dbatch, 2, H_A * P_A, N_A)
    h0_lru = jnp.transpose(state_lru[:, 0].reshape(dbatch, 2, H_B, BW_B), (0, 2, 1, 3))
    xd, _, tile_d = _layer0(x_sample.reshape(n_dec, D_MODEL), mod[0], dec_row, norm_g[0], ev, dbatch, dseq,
                            h0_ssd, h0_lru, False)
    new_state_ssd, new_state_lru = finals

    rope = (od["wqs"],) + _rope_tables(dseq)
    o_c, ckv_c, kr_c = _layer1_pre(xc, mod[1], tile_c, norm_g[1], od, batch, seq, None, None)
    o_d, _, _ = _layer1_pre(xd, mod[1], tile_d, norm_g[1], od, dbatch, dseq, rope,
                            (cache_ckv[:, 0], cache_krope[:, 0]))
    new_cache_ckv = ckv_c.reshape(batch, 1, seq, KV_RANK)
    new_cache_krope = kr_c.reshape(batch, 1, seq, ROPE_C)

    tiles_c = n_ctx // TM
    tile_all = lambda i: jnp.where(i < tiles_c, 0, tile_d(jnp.maximum(i - tiles_c, 0)))
    x3, h2, route, counts = _od_out_router(o_c, o_d, xc, xd, od["wo"], mod[1], tile_all, norm_g[1][1:2], od["wr"])
    y2 = _routed_experts(h2, route, counts, wgu_e, wd_e)
    n_tok = n_ctx + n_dec
    fg = final_g[None, :]
    y_prompt = _combine(x3, y2, route, 0, n_ctx, n_tok, mod[1], tile_all, fg).reshape(batch, seq, D_MODEL)
    y_sample = _combine(x3, y2, route, n_ctx, n_dec, n_tok, mod[1], tile_all, fg).reshape(dbatch, dseq, D_MODEL)
    return (y_prompt, y_sample, new_state_ssd, new_state_lru, new_cache_ckv, new_cache_krope)


def _routed_experts(h2, route, counts, wgu_e, wd_e):
    n_tok = route.shape[0]
    n_slots = (2 * n_tok // TM_E + N_EXPERTS) * TM_E
    cnt_e = counts[0, :N_EXPERTS].astype(jnp.int32)
    tiles_e = (cnt_e + TM_E - 1) // TM_E
    tile_end = jnp.cumsum(tiles_e)
    slot_start = (tile_end - tiles_e) * TM_E
    n_tiles = tile_end[-1:].astype(jnp.int32)
    tile_id = jnp.arange(n_slots // TM_E, dtype=jnp.int32)
    tile_expert = jnp.minimum(jnp.sum((tile_end[None, :] <= tile_id[:, None]).astype(jnp.int32), axis=1),
                              N_EXPERTS - 1)
    idx = route[:, ROUTE_IDX0:ROUTE_IDX1 + 1].astype(jnp.int32)
    rank = route[:, ROUTE_RANK0:ROUTE_RANK1 + 1].astype(jnp.int32)
    slot = slot_start[idx] + rank
    out_tok = jnp.arange(n_tok, dtype=jnp.int32)[:, None] + jnp.array([0, n_tok], jnp.int32)[None, :]
    spare = 2 * n_tok + jnp.arange(n_slots, dtype=jnp.int32) % TM_E
    slot_dst = spare.at[slot.reshape(-1)].set(out_tok.reshape(-1))
    slot_src = jnp.where(slot_dst >= 2 * n_tok, 0, jnp.where(slot_dst >= n_tok, slot_dst - n_tok, slot_dst))
    return _experts(tile_expert, n_tiles, slot_src * TOKEN_ROWS, slot_dst * TOKEN_ROWS, h2, wgu_e, wd_e, 2 * n_tok)
```

```python
import functools
import math

import jax
import jax.numpy as jnp
from jax import lax
from jax.experimental import pallas as pl
from jax.experimental.pallas import tpu as pltpu

F32 = jnp.float32
BF16 = jnp.bfloat16
HIGHEST = lax.Precision.HIGHEST

D_MODEL = 1024
GRID_W = 64
P_A = 64
H_A = 16
G_A = 2
N_A = 128
CHUNK = 128
CONV_K = 4
H_B = 8
BW_B = 128
LRU_C = 8.0
H_C = 16
Q_RANK = 384
KV_RANK = 256
NOPE_C = 64
ROPE_C = 32
V_C = 64
N_FREQ = ROPE_C // 4
ROPE_BASE = 10000.0
D_FF = 2816
N_EXPERTS = 8
D_FF_E = 3584
EPS = 1e-6

LANES = 128
HEAD_PAD = 128
ROW_BLOCK = 2048
ROW_CHUNK = 256
SEG_PITCH = ROW_CHUNK + 8
TM = 512
TQ_MAX = 256
LOG2_E = 1.4426950408889634
TM_E = 1024
FF_CHUNK_E = 512
VMEM_LIMIT = 52 * 1024 * 1024


def _cparams(n_axes, vmem=VMEM_LIMIT):
    return pltpu.CompilerParams(dimension_semantics=("arbitrary",) * n_axes, vmem_limit_bytes=vmem)


def _dot(a, b):
    return jnp.dot(a, b, preferred_element_type=F32)


def _sigmoid(x):
    return 1.0 / (1.0 + jnp.exp(-x))


def _silu(x):
    return x * _sigmoid(x)


def _softplus(x):
    return jnp.maximum(x, 0.0) + jnp.log(1.0 + jnp.exp(-jnp.abs(x)))


def _gelu_tanh(x):
    return 0.5 * x * (1.0 + jnp.tanh(math.sqrt(2.0 / math.pi) * (x + 0.044715 * (x * x * x))))


def _modulated_norm(x, gain, scale, shift):
    ms = jnp.mean(x * x, axis=-1, keepdims=True)
    return (x * lax.rsqrt(ms + EPS)) * (gain * (1.0 + scale)) + shift


def _const_spec(shape):
    zeros = (0,) * len(shape)
    return pl.BlockSpec(shape, lambda *_: zeros, pipeline_mode=pl.Buffered(1))


def _mod_kernel(c_ref, w_ref, b_ref, o_ref):
    c = c_ref[...]
    o_ref[0] = _dot(_silu(c).astype(BF16), w_ref[0].astype(BF16)) + b_ref[0]


def _modulation(cond, mod_w, mod_b):
    depth = mod_w.shape[0]
    n = 6 * D_MODEL
    tn = n // 4
    out = pl.pallas_call(
        _mod_kernel,
        out_shape=jax.ShapeDtypeStruct((depth, 16, n), F32),
        grid=(depth, 4),
        in_specs=[pl.BlockSpec((16, D_MODEL), lambda l, j: (0, 0)),
                  pl.BlockSpec((1, D_MODEL, tn), lambda l, j: (l, 0, j)),
                  pl.BlockSpec((1, 1, tn), lambda l, j: (l, 0, j))],
        out_specs=pl.BlockSpec((1, 16, tn), lambda l, j: (l, 0, j)),
        compiler_params=_cparams(2),
        name="modulation",
    )(cond, mod_w, mod_b.reshape(depth, 1, n))
    return out.reshape(depth, 16, 6, D_MODEL)


def _ev_inproj_kernel(x_ref, mod_ref, g_ref, w_ref, wdt_ref, cw_ref, cb_ref, dtb_ref,
                      p_ref, dt_ref, h_scr, acc_scr, *, seq_is_chunk):
    j = pl.program_id(1)
    rows = h_scr.shape[0]
    n_chunks = rows // ROW_CHUNK
    width = w_ref.shape[1]

    @pl.when(j == 0)
    def _():
        gain, scale, shift = g_ref[...], mod_ref[0, 1:2, :], mod_ref[0, 0:1, :]

        def body(r, carry):
            r0 = pl.multiple_of(r * ROW_CHUNK, ROW_CHUNK)
            h = _modulated_norm(x_ref[pl.ds(r0, ROW_CHUNK), :], gain, scale, shift)
            h_scr[pl.ds(r0, ROW_CHUNK), :] = h.astype(BF16)
            return carry

        lax.fori_loop(0, n_chunks, body, 0)
        dt = _softplus(_dot(h_scr[...], wdt_ref[...]) + dtb_ref[...])
        lane = lax.broadcasted_iota(jnp.int32, dt.shape, 1)
        dt_ref[...] = jnp.where(lane < 2 * H_A, dt, 0.0)
        acc_scr[0:8, :] = jnp.zeros((8, width), F32)
        acc_scr[rows + 8:rows + 16, :] = jnp.zeros((8, width), F32)

    acc_scr[8:rows + 8, :] = _dot(h_scr[...], w_ref[...])

    def plain(act):
        def body(r, carry):
            r0 = pl.multiple_of(r * ROW_CHUNK, ROW_CHUNK)
            v = acc_scr[pl.ds(r0 + 8, ROW_CHUNK), :]
            p_ref[pl.ds(r0, ROW_CHUNK), :] = act(v).astype(BF16)
            return carry
        lax.fori_loop(0, n_chunks, body, 0)

    def conv(act):
        win = ROW_CHUNK + 16
        cw = cw_ref[...]
        cb = cb_ref[...]

        def body(r, carry):
            r0 = pl.multiple_of(r * ROW_CHUNK, ROW_CHUNK)
            xw = acc_scr[pl.ds(r0, win), :]
            if seq_is_chunk:
                row = lax.broadcasted_iota(jnp.int32, xw.shape, 0)
                xw = jnp.where((row >= 8) & (row < ROW_CHUNK + 8), xw, 0.0)
            y = cb + cw[1:2, :] * xw[8:ROW_CHUNK + 8]
            for k in (0, 2, 3):
                sh = pltpu.roll(xw, (1 - k) % win, 0)[8:ROW_CHUNK + 8]
                y = y + cw[k:k + 1, :] * sh
            p_ref[pl.ds(r0, ROW_CHUNK), :] = act(y).astype(BF16)
            return carry
        lax.fori_loop(0, n_chunks, body, 0)

    @pl.when(j < 2)
    def _():
        plain(lambda v: v)

    @pl.when((j >= 2) & (j < 4))
    def _():
        conv(lambda v: v)

    @pl.when((j >= 4) & (j < 6))
    def _():
        plain(_gelu_tanh)

    @pl.when(j >= 6)
    def _():
        conv(_silu)


P_COLS = 4608


def _ev_inproj(x2d, mod_l, cond_of_block, gain, w, wdt, cw, cb, dtb, seq_len):
    n = x2d.shape[0]
    nblk = n // ROW_BLOCK
    tn = 512
    kern = functools.partial(_ev_inproj_kernel, seq_is_chunk=(seq_len == ROW_CHUNK))
    return pl.pallas_call(
        kern,
        out_shape=(jax.ShapeDtypeStruct((n, P_COLS), BF16), jax.ShapeDtypeStruct((n, LANES), F32)),
        grid=(nblk, P_COLS // tn),
        in_specs=[pl.BlockSpec((ROW_BLOCK, D_MODEL), lambda i, j: (i, 0)),
                  pl.BlockSpec((1, 6, D_MODEL), lambda i, j: (cond_of_block(i), 0, 0)),
                  pl.BlockSpec((1, D_MODEL), lambda i, j: (0, 0)),
                  pl.BlockSpec((D_MODEL, tn), lambda i, j: (0, j)),
                  pl.BlockSpec((D_MODEL, LANES), lambda i, j: (0, 0)),
                  pl.BlockSpec((CONV_K, tn), lambda i, j: (0, j)),
                  pl.BlockSpec((1, tn), lambda i, j: (0, j)),
                  pl.BlockSpec((1, LANES), lambda i, j: (0, 0))],
        out_specs=(pl.BlockSpec((ROW_BLOCK, tn), lambda i, j: (i, j)),
                   pl.BlockSpec((ROW_BLOCK, LANES), lambda i, j: (i, 0))),
        scratch_shapes=[pltpu.VMEM((ROW_BLOCK, D_MODEL), BF16),
                        pltpu.VMEM((ROW_BLOCK + 16, tn), F32)],
        compiler_params=_cparams(2),
        name="ev_inproj",
    )(x2d, mod_l, gain, w, wdt, cw, cb, dtb)


def _ssd_direction(d, xs_ref, bm_ref, cm_ref, dt_ref, arate, e2, h_scr, y_ref):
    xs = xs_ref[...].astype(F32)
    bm = bm_ref[...]
    cm = cm_ref[...]
    a = dt_ref[...] * arate
    li = lax.broadcasted_iota(jnp.int32, (CHUNK, CHUNK), 0)
    si = lax.broadcasted_iota(jnp.int32, (CHUNK, CHUNK), 1)
    keep = (si <= li) if d == 0 else (si >= li)
    tri = jnp.where(keep, 1.0, 0.0).astype(F32)
    acs = jnp.dot(tri, a, precision=HIGHEST, preferred_element_type=F32)
    acs_t = acs.T
    tot = acs[CHUNK - 1:CHUNK, :] if d == 0 else acs[0:1, :]
    stack = jnp.concatenate([dt_ref[...], jnp.exp(acs), jnp.exp(tot - acs)], axis=0)
    hi = stack.astype(BF16)
    lo = (stack - hi.astype(F32)).astype(BF16)
    ex = _dot(jnp.concatenate([hi, lo], axis=1), e2)
    dt_e, eacs_e, ds_e = ex[0:CHUNK], ex[CHUNK:2 * CHUNK], ex[2 * CHUNK:3 * CHUNK]
    xdt = xs * dt_e
    xdt_b = xdt.astype(BF16)
    xds_b = (xdt * ds_e).astype(BF16)
    lane = lax.broadcasted_iota(jnp.int32, (CHUNK, LANES), 1)
    gw = (H_A // G_A) * P_A
    y_groups = []
    for g in range(G_A):
        bm_g = bm[:, g * N_A:(g + 1) * N_A]
        cm_g = cm[:, g * N_A:(g + 1) * N_A]
        cb = lax.dot_general(cm_g, bm_g, (((1,), (1,)), ((), ())), preferred_element_type=F32)
        h_prev = h_scr[d, :, g * gw:(g + 1) * gw]
        y_off = _dot(cm_g, h_prev.astype(BF16)) * eacs_e[:, g * gw:(g + 1) * gw]
        bm_t = bm_g.astype(F32).T.astype(BF16)
        st = _dot(bm_t, xds_b[:, g * gw:(g + 1) * gw])
        decay_tot = eacs_e[CHUNK - 1:CHUNK, g * gw:(g + 1) * gw] if d == 0 else eacs_e[0:1, g * gw:(g + 1) * gw]
        h_scr[d, :, g * gw:(g + 1) * gw] = decay_tot * h_prev + st
        pairs = []
        for q in range(H_A // G_A // 2):
            ms = []
            for e in (g * 8 + 2 * q, g * 8 + 2 * q + 1):
                col = acs[:, d * H_A + e:d * H_A + e + 1]
                row = acs_t[d * H_A + e:d * H_A + e + 1, :]
                dec = jnp.where(keep, jnp.exp(jnp.minimum(col - row, 0.0)), 0.0)
                ms.append((cb * dec).astype(BF16))
            lhs = jnp.concatenate(ms, axis=1)
            xp = xdt_b[:, (g * 8 + 2 * q) * P_A:(g * 8 + 2 * q + 2) * P_A]
            zero = jnp.zeros_like(xp)
            rhs = jnp.concatenate([jnp.where(lane < P_A, xp, zero), jnp.where(lane >= P_A, xp, zero)], axis=0)
            pairs.append(_dot(lhs, rhs))
        y_groups.append(jnp.concatenate(pairs, axis=1) + y_off)
    y_ref[...] = jnp.concatenate(y_groups, axis=1).astype(y_ref.dtype)


def _ssd_kernel(*refs, has_h0, emit_final):
    (xsf, bmf, cmf, dtf, xsb, bmb, cmb, dtb, alog_ref, e2_ref), rest = refs[:10], refs[10:]
    if has_h0:
        h0_ref, rest = rest[0], rest[1:]
    yf_ref, yb_ref, rest = rest[0], rest[1], rest[2:]
    if emit_final:
        hfin_ref, rest = rest[0], rest[1:]
    h_scr = rest[0]
    i = pl.program_id(1)

    @pl.when(i == 0)
    def _():
        for d in range(2):
            if has_h0:
                h_scr[d] = h0_ref[0, d].T
            else:
                h_scr[d] = jnp.zeros(h_scr.shape[1:], F32)

    arate = -jnp.exp(alog_ref[...])
    _ssd_direction(0, xsf, bmf, cmf, dtf, arate, e2_ref[0], h_scr, yf_ref)
    _ssd_direction(1, xsb, bmb, cmb, dtb, arate, e2_ref[1], h_scr, yb_ref)

    if emit_final:
        @pl.when(i == pl.num_programs(1) - 1)
        def _():
            for d in range(2):
                hfin_ref[0, d] = h_scr[d].T


def _ssd(p, dt, alog_row, e2, h0, n_batch, seq_len, emit_final):
    n = p.shape[0]
    nc = seq_len // CHUNK
    hp = H_A * P_A
    fwd = lambda b, i: b * nc + i
    bwd = lambda b, i: b * nc + (nc - 1 - i)
    xs_blk, bm_blk, cm_blk = 3, 16, 17

    def specs(rowf):
        return [pl.BlockSpec((CHUNK, hp), lambda b, i: (rowf(b, i), xs_blk)),
                pl.BlockSpec((CHUNK, G_A * N_A), lambda b, i: (rowf(b, i), bm_blk)),
                pl.BlockSpec((CHUNK, G_A * N_A), lambda b, i: (rowf(b, i), cm_blk)),
                pl.BlockSpec((CHUNK, LANES), lambda b, i: (rowf(b, i), 0))]

    in_specs = specs(fwd) + specs(bwd) + [pl.BlockSpec((1, LANES), lambda b, i: (0, 0)),
                                          pl.BlockSpec((2, 2 * LANES, hp), lambda b, i: (0, 0, 0))]
    args = [p, p, p, dt, p, p, p, dt, alog_row, e2]
    if h0 is not None:
        in_specs.append(pl.BlockSpec((1, 2, hp, N_A), lambda b, i: (b, 0, 0, 0)))
        args.append(h0)
    out_shape = [jax.ShapeDtypeStruct((n, hp), BF16), jax.ShapeDtypeStruct((n, hp), BF16)]
    out_specs = [pl.BlockSpec((CHUNK, hp), lambda b, i: (fwd(b, i), 0)),
                 pl.BlockSpec((CHUNK, hp), lambda b, i: (bwd(b, i), 0))]
    if emit_final:
        out_shape.append(jax.ShapeDtypeStruct((n_batch, 2, hp, N_A), F32))
        out_specs.append(pl.BlockSpec((1, 2, hp, N_A), lambda b, i: (b, 0, 0, 0)))
    kern = functools.partial(_ssd_kernel, has_h0=h0 is not None, emit_final=emit_final)
    return pl.pallas_call(
        kern, out_shape=tuple(out_shape), grid=(n_batch, nc),
        in_specs=in_specs, out_specs=tuple(out_specs),
        scratch_shapes=[pltpu.VMEM((2, N_A, hp), F32)],
        compiler_params=_cparams(2),
        name="ssd",
    )(*args)


def _lru_kernel(*refs, chain, emit_final):
    (xr_ref, gg_ref, wl_ref, lb_ref, lam_ref), rest = refs[:5], refs[5:]
    if chain:
        h0_ref, rest = rest[0], rest[1:]
    y_ref, rest = rest[0], rest[1:]
    if emit_final:
        fin_ref, rest = rest[0], rest[1:]
    a_scr, u_scr, h_scr, p_scr = rest
    rows = xr_ref.shape[0]
    n_seg = rows // ROW_CHUNK
    sp = _softplus(-lam_ref[0])

    def gates(r, carry):
        r0 = pl.multiple_of(r * ROW_CHUNK, ROW_CHUNK)
        s0 = pl.multiple_of(r * SEG_PITCH, 8)
        xr_b = xr_ref[pl.ds(r0, ROW_CHUNK), :]
        xr = xr_b.astype(F32)
        g = _dot(xr_b, wl_ref[0]) + lb_ref[0]
        for d in range(2):
            r_gate = _sigmoid(g[:, (2 * d) * BW_B:(2 * d + 1) * BW_B])
            i_gate = _sigmoid(g[:, (2 * d + 1) * BW_B:(2 * d + 2) * BW_B])
            a = jnp.exp((-LRU_C) * r_gate * sp[d:d + 1, :])
            u = jnp.sqrt(jnp.maximum(1.0 - a * a, 0.0)) * (i_gate * xr)
            a_scr[d, pl.ds(s0, ROW_CHUNK), :] = a
            u_scr[d, pl.ds(s0, ROW_CHUNK), :] = u
        return carry

    lax.fori_loop(0, n_seg, gates, 0)

    def step(t, carry):
        hf, hb, pf, pb = carry
        tb = ROW_CHUNK - 1 - t
        sf = pl.ds(t, n_seg, stride=SEG_PITCH)
        sb = pl.ds(tb, n_seg, stride=SEG_PITCH)
        af = a_scr[0, sf, :]
        ab = a_scr[1, sb, :]
        hf = af * hf + u_scr[0, sf, :]
        hb = ab * hb + u_scr[1, sb, :]
        h_scr[0, sf, :] = hf
        h_scr[1, sb, :] = hb
        if chain:
            pf = af * pf
            pb = ab * pb
            p_scr[0, sf, :] = pf
            p_scr[1, sb, :] = pb
        return hf, hb, pf, pb

    z = jnp.zeros((n_seg, BW_B), F32)
    o = jnp.ones((n_seg, BW_B), F32)
    lax.fori_loop(0, ROW_CHUNK, step, (z, z, o, o), unroll=8)

    if emit_final:
        fin_ref[0, 0, 0] = h_scr[0, pl.ds(ROW_CHUNK - 1, n_seg, stride=SEG_PITCH), :]
        fin_ref[0, 0, 1] = h_scr[1, pl.ds(0, n_seg, stride=SEG_PITCH), :]

    if chain:
        carry = h0_ref[0, 0, 0:1, :]
        for s in range(n_seg):
            sl = slice(s * SEG_PITCH, s * SEG_PITCH + ROW_CHUNK)
            h = h_scr[0, sl, :] + p_scr[0, sl, :] * carry
            h_scr[0, sl, :] = h
            carry = h[ROW_CHUNK - 1:ROW_CHUNK, :]
        carry = h0_ref[0, 0, 1:2, :]
        for s in reversed(range(n_seg)):
            sl = slice(s * SEG_PITCH, s * SEG_PITCH + ROW_CHUNK)
            h = h_scr[1, sl, :] + p_scr[1, sl, :] * carry
            h_scr[1, sl, :] = h
            carry = h[0:1, :]

    for s in range(n_seg):
        sl = slice(s * SEG_PITCH, s * SEG_PITCH + ROW_CHUNK)
        gg = gg_ref[s * ROW_CHUNK:(s + 1) * ROW_CHUNK, :].astype(F32)
        y_ref[s * ROW_CHUNK:(s + 1) * ROW_CHUNK, :] = (gg * (h_scr[0, sl, :] + h_scr[1, sl, :])).astype(BF16)


def _lru(p, wl, lb, lam, h0, chain, emit_final):
    n = p.shape[0]
    nblk = n // ROW_BLOCK
    n_seg = ROW_BLOCK // ROW_CHUNK
    xb_blk, gate_blk = D_MODEL // BW_B, 2 * D_MODEL // BW_B
    in_specs = [pl.BlockSpec((ROW_BLOCK, BW_B), lambda i, h: (i, xb_blk + h)),
                pl.BlockSpec((ROW_BLOCK, BW_B), lambda i, h: (i, gate_blk + h)),
                pl.BlockSpec((1, BW_B, 4 * BW_B), lambda i, h: (h, 0, 0)),
                pl.BlockSpec((1, 1, 4 * BW_B), lambda i, h: (h, 0, 0)),
                pl.BlockSpec((1, 2, BW_B), lambda i, h: (h, 0, 0))]
    args = [p, p, wl, lb, lam]
    if chain:
        in_specs.append(pl.BlockSpec((1, 1, 2, BW_B), lambda i, h: (i, h, 0, 0)))
        args.append(h0)
    out_shape = [jax.ShapeDtypeStruct((n, D_MODEL), BF16)]
    out_specs = [pl.BlockSpec((ROW_BLOCK, BW_B), lambda i, h: (i, h))]
    if emit_final:
        out_shape.append(jax.ShapeDtypeStruct((nblk, H_B, 2, n_seg, BW_B), F32))
        out_specs.append(pl.BlockSpec((1, 1, 2, n_seg, BW_B), lambda i, h: (i, h, 0, 0, 0)))
    seg_rows = n_seg * SEG_PITCH
    kern = functools.partial(_lru_kernel, chain=chain, emit_final=emit_final)
    return pl.pallas_call(
        kern, out_shape=tuple(out_shape), grid=(nblk, H_B),
        in_specs=in_specs, out_specs=tuple(out_specs),
        scratch_shapes=[pltpu.VMEM((2, seg_rows, BW_B), F32)] * 4,
        compiler_params=_cparams(2),
        name="lru",
    )(*args)


def _ev_out_kernel(x_ref, yf_ref, yb_ref, xs_ref, z_ref, yl_ref, wo_ref, dsk_ref, gn_ref, mod_ref, o_ref):
    y = yf_ref[...].astype(F32) + yb_ref[...].astype(F32) + dsk_ref[...] * xs_ref[...].astype(F32)
    y = y * _silu(z_ref[...].astype(F32))
    gw = D_MODEL // G_A
    parts = []
    for g in range(G_A):
        yg = y[:, g * gw:(g + 1) * gw]
        ms = jnp.mean(yg * yg, axis=-1, keepdims=True)
        parts.append((yg * lax.rsqrt(ms + EPS) * gn_ref[:, g * gw:(g + 1) * gw]).astype(BF16))
    mix = _dot(parts[0], wo_ref[0:gw, :]) + _dot(parts[1], wo_ref[gw:2 * gw, :])
    mix = mix + _dot(yl_ref[...], wo_ref[D_MODEL:2 * D_MODEL, :])
    o_ref[...] = x_ref[...] + mod_ref[0, 2:3, :] * mix


def _ev_out(x2d, yf, yb, p, ylru, wo, dsk, gn, mod_l, cond_of_tile):
    n = x2d.shape[0]
    row = lambda i: (i, 0)
    return pl.pallas_call(
        _ev_out_kernel,
        out_shape=jax.ShapeDtypeStruct((n, D_MODEL), F32),
        grid=(n // TM,),
        in_specs=[pl.BlockSpec((TM, D_MODEL), row), pl.BlockSpec((TM, D_MODEL), row),
                  pl.BlockSpec((TM, D_MODEL), row),
                  pl.BlockSpec((TM, D_MODEL), lambda i: (i, 3)),
                  pl.BlockSpec((TM, D_MODEL), lambda i: (i, 0)),
                  pl.BlockSpec((TM, D_MODEL), row),
                  _const_spec((2 * D_MODEL, D_MODEL)), _const_spec((1, D_MODEL)), _const_spec((1, D_MODEL)),
                  pl.BlockSpec((1, 6, D_MODEL), lambda i: (cond_of_tile(i), 0, 0))],
        out_specs=pl.BlockSpec((TM, D_MODEL), row),
        compiler_params=_cparams(1),
        name="ev_out",
    )(x2d, yf, yb, p, p, ylru, wo, dsk, gn, mod_l)


def _ffn_kernel(x_ref, mod_ref, g_ref, wgu_ref, wd_ref, o_ref):
    x = x_ref[...]
    h = _modulated_norm(x, g_ref[...], mod_ref[0, 4:5, :], mod_ref[0, 3:4, :]).astype(BF16)
    n_chunk = 2
    fc = D_FF // n_chunk
    f = None
    for c in range(n_chunk):
        g = _dot(h, wgu_ref[:, c * fc:(c + 1) * fc])
        u = _dot(h, wgu_ref[:, D_FF + c * fc:D_FF + (c + 1) * fc])
        part = _dot((_silu(g) * u).astype(BF16), wd_ref[c * fc:(c + 1) * fc, :])
        f = part if f is None else f + part
    o_ref[...] = x + mod_ref[0, 5:6, :] * f


def _ffn(x2d, mod_l, cond_of_tile, gain, wgu, wd):
    n = x2d.shape[0]
    return pl.pallas_call(
        _ffn_kernel,
        out_shape=jax.ShapeDtypeStruct((n, D_MODEL), F32),
        grid=(n // TM,),
        in_specs=[pl.BlockSpec((TM, D_MODEL), lambda i: (i, 0)),
                  pl.BlockSpec((1, 6, D_MODEL), lambda i: (cond_of_tile(i), 0, 0)),
                  _const_spec((1, D_MODEL)), _const_spec((D_MODEL, 2 * D_FF)), _const_spec((D_FF, D_MODEL))],
        out_specs=pl.BlockSpec((TM, D_MODEL), lambda i: (i, 0)),
        compiler_params=_cparams(1),
        name="ffn",
    )(x2d, mod_l, gain, wgu, wd)


OD_IN_COLS = 896


def _od_inproj_kernel(*refs, use_rope):
    (x_ref, mod_ref, g_ref, wi_ref, qn_ref, kvn_ref, wq_ref), rest = refs[:7], refs[7:]
    if use_rope:
        (wqs_ref, cq_ref, sq_ref, ck_ref, sk_ref), rest = rest[:5], rest[5:]
    q_ref, ckv_ref, kr_ref = rest
    h = _modulated_norm(x_ref[...], g_ref[...], mod_ref[0, 1:2, :], mod_ref[0, 0:1, :]).astype(BF16)
    proj = _dot(h, wi_ref[...])
    cq = proj[:, 0:Q_RANK]
    cqn = (cq * lax.rsqrt(jnp.mean(cq * cq, axis=-1, keepdims=True) + EPS) * qn_ref[...]).astype(BF16)
    ckv = proj[:, Q_RANK:Q_RANK + KV_RANK]
    ckv_ref[...] = ckv * lax.rsqrt(jnp.mean(ckv * ckv, axis=-1, keepdims=True) + EPS) * kvn_ref[...]
    kr = proj[:, 640:640 + ROPE_C]
    q = _dot(cqn, wq_ref[...])
    if use_rope:
        q = q * jnp.tile(cq_ref[...], (1, H_C)) + _dot(cqn, wqs_ref[...]) * jnp.tile(sq_ref[...], (1, H_C))
        kr = kr * ck_ref[...] + proj[:, 768:768 + ROPE_C] * sk_ref[...]
    q = (q * ((NOPE_C + ROPE_C) ** -0.5 * LOG2_E)).astype(BF16)
    for h in range(H_C):
        q_ref[h] = q[:, h * HEAD_PAD:(h + 1) * HEAD_PAD]
    kr_ref[...] = kr


def _od_inproj(x2d, mod_l, cond_of_tile, gain, wi, qn, kvn, wq, rope, seq_len):
    n = x2d.shape[0]
    use_rope = rope is not None
    in_specs = [pl.BlockSpec((TM, D_MODEL), lambda i: (i, 0)),
                pl.BlockSpec((1, 6, D_MODEL), lambda i: (cond_of_tile(i), 0, 0)),
                _const_spec((1, D_MODEL)), _const_spec((D_MODEL, OD_IN_COLS)),
                _const_spec((1, Q_RANK)), _const_spec((1, KV_RANK)),
                _const_spec((Q_RANK, H_C * HEAD_PAD))]
    args = [x2d, mod_l, gain, wi, qn, kvn, wq]
    if use_rope:
        wqs, cos_q, sin_q, cos_k, sin_k = rope
        per_seq = seq_len // TM
        pos = lambda i: (i % per_seq, 0)
        in_specs += [_const_spec((Q_RANK, H_C * HEAD_PAD)),
                     pl.BlockSpec((TM, HEAD_PAD), pos), pl.BlockSpec((TM, HEAD_PAD), pos),
                     pl.BlockSpec((TM, ROPE_C), pos), pl.BlockSpec((TM, ROPE_C), pos)]
        args += [wqs, cos_q, sin_q, cos_k, sin_k]
    return pl.pallas_call(
        functools.partial(_od_inproj_kernel, use_rope=use_rope),
        out_shape=(jax.ShapeDtypeStruct((H_C, n, HEAD_PAD), BF16),
                   jax.ShapeDtypeStruct((n, KV_RANK), F32),
                   jax.ShapeDtypeStruct((n, ROPE_C), F32)),
        grid=(n // TM,),
        in_specs=in_specs,
        out_specs=(pl.BlockSpec((H_C, TM, HEAD_PAD), lambda i: (0, i, 0)),
                   pl.BlockSpec((TM, KV_RANK), lambda i: (i, 0)),
                   pl.BlockSpec((TM, ROPE_C), lambda i: (i, 0))),
        compiler_params=_cparams(1),
        name="od_inproj",
    )(*args)


def _kv_up_kernel(ckv_ref, kr_ref, wk_ref, ek_ref, wv_ref, k_ref, v_ref):
    ckv = ckv_ref[...].astype(BF16)
    k = (_dot(ckv, wk_ref[...]) + _dot(kr_ref[...].astype(BF16), ek_ref[...])).astype(BF16)
    v_t = lax.dot_general(wv_ref[...], ckv, (((1,), (1,)), ((), ())), preferred_element_type=F32)
    for h in range(H_C):
        k_ref[h] = k[:, h * HEAD_PAD:(h + 1) * HEAD_PAD]
    v_ref[...] = v_t.astype(BF16)


def _kv_up(ckv_all, kr_all, wk, ek, wv_t):
    m = ckv_all.shape[0]
    return pl.pallas_call(
        _kv_up_kernel,
        out_shape=(jax.ShapeDtypeStruct((H_C, m, HEAD_PAD), BF16),
                   jax.ShapeDtypeStruct((H_C * V_C, m), BF16)),
        grid=(m // TM,),
        in_specs=[pl.BlockSpec((TM, KV_RANK), lambda i: (i, 0)), pl.BlockSpec((TM, ROPE_C), lambda i: (i, 0)),
                  _const_spec((KV_RANK, H_C * HEAD_PAD)), _const_spec((ROPE_C, H_C * HEAD_PAD)),
                  _const_spec((H_C * V_C, KV_RANK))],
        out_specs=(pl.BlockSpec((H_C, TM, HEAD_PAD), lambda i: (0, i, 0)),
                   pl.BlockSpec((H_C * V_C, TM), lambda i: (0, i))),
        compiler_params=_cparams(1),
        name="kv_up",
    )(ckv_all, kr_all, wk, ek, wv_t)


def _attn_kernel(q_ref, k_ref, v_ref, o_ref, sa_ref, sb_ref):
    def scores(h, s_ref):
        s_ref[...] = lax.dot_general(k_ref[h], q_ref[h], (((1,), (1,)), ((), ())), preferred_element_type=F32)

    def over_keys(x, reduce):
        slabs = x.reshape(x.shape[0] // LANES, LANES, x.shape[1])
        return reduce(reduce(slabs, axis=0), axis=0, keepdims=True)

    def weighted_values(h, s_ref):
        s_t = s_ref[...]
        p_t = jnp.exp2(s_t - over_keys(s_t, jnp.max))
        l = over_keys(p_t, jnp.sum)
        v_t = v_ref[pl.ds(pl.multiple_of(h * V_C, V_C), V_C), :]
        return _dot(v_t, p_t.astype(BF16)) / l

    scores(0, sa_ref)

    def head_pair(hp, carry):
        h = 2 * hp
        scores(h + 1, sb_ref)
        o_a = weighted_values(h, sa_ref)
        scores(jnp.minimum(h + 2, H_C - 1), sa_ref)
        o_b = weighted_values(h + 1, sb_ref)
        o_ref[hp] = jnp.concatenate([o_a, o_b], axis=0).T.astype(BF16)
        return carry

    lax.fori_loop(0, H_C // 2, head_pair, 0)


def _attention(q, k, v_t, n_batch, seq_len, kv_len):
    tq = min(seq_len, TQ_MAX)
    per = seq_len // tq
    return pl.pallas_call(
        _attn_kernel,
        out_shape=jax.ShapeDtypeStruct((H_C // 2, q.shape[1], LANES), BF16),
        grid=(n_batch, per),
        in_specs=[pl.BlockSpec((H_C, tq, HEAD_PAD), lambda b, i: (0, b * per + i, 0)),
                  pl.BlockSpec((H_C, kv_len, HEAD_PAD), lambda b, i: (0, b, 0)),
                  pl.BlockSpec((H_C * V_C, kv_len), lambda b, i: (0, b))],
        out_specs=pl.BlockSpec((H_C // 2, tq, LANES), lambda b, i: (0, b * per + i, 0)),
        scratch_shapes=[pltpu.VMEM((kv_len, tq), F32)] * 2,
        compiler_params=_cparams(2),
        name="attention",
    )(q, k, v_t)


TOKEN_ROWS = D_MODEL // LANES


def _store_token_rows(ref, value, index=()):
    for s in range(TOKEN_ROWS):
        ref[index + (pl.ds(s, value.shape[0], stride=TOKEN_ROWS), slice(None))] = value[:, s * LANES:(s + 1) * LANES]


def _load_token_rows(ref, rows, index=()):
    parts = [ref[index + (pl.ds(s, rows, stride=TOKEN_ROWS), slice(None))] for s in range(TOKEN_ROWS)]
    return jnp.concatenate(parts, axis=-1)


ROUTE_IDX0, ROUTE_IDX1, ROUTE_RANK0, ROUTE_RANK1, ROUTE_W0, ROUTE_W1 = range(6)


def _od_out_router_kernel(oa_ref, ob_ref, xa_ref, xb_ref, wo_ref, mod_ref, g_ref, wr_ref,
                          x3_ref, h2_ref, route_ref, cnt_ref, cnt_scr, *, tiles_a):
    i = pl.program_id(0)

    @pl.when(i == 0)
    def _():
        cnt_scr[...] = jnp.zeros(cnt_scr.shape, F32)

    from_a = i < tiles_a
    o = jnp.concatenate([jnp.where(from_a, oa_ref[hp], ob_ref[hp]) for hp in range(H_C // 2)], axis=-1)
    x3 = jnp.where(from_a, xa_ref[...], xb_ref[...]) + mod_ref[0, 2:3, :] * _dot(o, wo_ref[...])
    x3_ref[...] = x3
    h2 = _modulated_norm(x3, g_ref[...], mod_ref[0, 4:5, :], mod_ref[0, 3:4, :])
    _store_token_rows(h2_ref, h2)
    h_hi = h2.astype(BF16)
    h_lo = (h2 - h_hi.astype(F32)).astype(BF16)
    w_hi, w_lo = wr_ref[0], wr_ref[1]
    logits = _dot(h_hi, w_hi) + (_dot(h_lo, w_hi) + _dot(h_hi, w_lo))
    tm = logits.shape[0]
    lane = lax.broadcasted_iota(jnp.int32, logits.shape, 1)
    neg = jnp.float32(-jnp.inf)
    logits = jnp.where(lane < N_EXPERTS, logits, neg)
    m0 = jnp.max(logits, axis=-1, keepdims=True)
    lane_f = lane.astype(F32)
    i0 = jnp.min(jnp.where(logits == m0, lane_f, float(LANES)), axis=-1, keepdims=True)
    rest = jnp.where(lane_f == i0, neg, logits)
    m1 = jnp.max(rest, axis=-1, keepdims=True)
    i1 = jnp.min(jnp.where(rest == m1, lane_f, float(LANES)), axis=-1, keepdims=True)
    e = jnp.exp(m1 - m0)
    w0 = 1.0 / (1.0 + e)
    w1 = e * w0
    oh0 = jnp.where(lane_f == i0, 1.0, 0.0).astype(F32)
    oh1 = jnp.where(lane_f == i1, 1.0, 0.0).astype(F32)
    sel = oh0 + oh1
    ri = lax.broadcasted_iota(jnp.int32, (tm, tm), 0)
    ci = lax.broadcasted_iota(jnp.int32, (tm, tm), 1)
    before = jnp.where(ci < ri, 1.0, 0.0).astype(BF16)
    prior = _dot(before, sel.astype(BF16)) + cnt_scr[0:1, :]
    r0 = jnp.sum(oh0 * prior, axis=-1, keepdims=True)
    r1 = jnp.sum(oh1 * prior, axis=-1, keepdims=True)
    route = jnp.where(lane == ROUTE_IDX0, i0, 0.0)
    route = jnp.where(lane == ROUTE_IDX1, i1, route)
    route = jnp.where(lane == ROUTE_RANK0, r0, route)
    route = jnp.where(lane == ROUTE_RANK1, r1, route)
    route = jnp.where(lane == ROUTE_W0, w0, route)
    route = jnp.where(lane == ROUTE_W1, w1, route)
    route_ref[...] = route
    total = cnt_scr[0:1, :] + jnp.sum(sel, axis=0, keepdims=True)
    cnt_scr[...] = jnp.broadcast_to(total, cnt_scr.shape)
    cnt_ref[...] = cnt_scr[...]


def _od_out_router(o_a, o_b, x_a, x_b, wo, mod_l, cond_of_tile, gain, wr):
    n_a, n_b = x_a.shape[0], x_b.shape[0]
    n = n_a + n_b
    tiles_a = n_a // TM
    row = lambda i: (i, 0)
    a_tile = lambda i: jnp.minimum(i, tiles_a - 1)
    b_tile = lambda i: jnp.maximum(i - tiles_a, 0)
    return pl.pallas_call(
        functools.partial(_od_out_router_kernel, tiles_a=tiles_a),
        out_shape=(jax.ShapeDtypeStruct((n, D_MODEL), F32), jax.ShapeDtypeStruct((n * TOKEN_ROWS, LANES), F32),
                   jax.ShapeDtypeStruct((n, LANES), F32), jax.ShapeDtypeStruct((8, LANES), F32)),
        grid=(n // TM,),
        in_specs=[pl.BlockSpec((H_C // 2, TM, LANES), lambda i: (0, a_tile(i), 0)),
                  pl.BlockSpec((H_C // 2, TM, LANES), lambda i: (0, b_tile(i), 0)),
                  pl.BlockSpec((TM, D_MODEL), lambda i: (a_tile(i), 0)),
                  pl.BlockSpec((TM, D_MODEL), lambda i: (b_tile(i), 0)),
                  _const_spec((D_MODEL, D_MODEL)),
                  pl.BlockSpec((1, 6, D_MODEL), lambda i: (cond_of_tile(i), 0, 0)),
                  _const_spec((1, D_MODEL)), _const_spec((2, D_MODEL, LANES))],
        out_specs=(pl.BlockSpec((TM, D_MODEL), row), pl.BlockSpec((TM * TOKEN_ROWS, LANES), row),
                   pl.BlockSpec((TM, LANES), row), pl.BlockSpec((8, LANES), lambda i: (0, 0))),
        scratch_shapes=[pltpu.VMEM((8, LANES), F32)],
        compiler_params=_cparams(1),
        name="od_out_router",
    )(o_a, o_b, x_a, x_b, wo, mod_l, gain, wr)


def _token_copy(src, s8, dst, d8, sem):
    return pltpu.make_async_copy(src.at[pl.ds(pl.multiple_of(s8, TOKEN_ROWS), TOKEN_ROWS)],
                                 dst.at[pl.ds(pl.multiple_of(d8, TOKEN_ROWS), TOKEN_ROWS)], sem)


ROW_DMA_UNROLL = 8
ROW_DMA_PRIORITY = 1


def _experts_kernel(te_ref, nt_ref, src0_ref, src1_ref, dst_ref, h_hbm, wg_ref, wu_ref, wd_ref, y_hbm,
                    h_buf, y_buf, h_scr, acc_scr, sem_in, sem_out):
    i, c = pl.program_id(0), pl.program_id(1)
    last_c = pl.num_programs(1) - 1
    n_tiles = nt_ref[0]
    tile_rows = TM_E * TOKEN_ROWS

    def gather(buf, src_ref):
        def body(r, carry):
            _token_copy(h_hbm, src_ref[r], h_buf.at[buf], r * TOKEN_ROWS, sem_in.at[buf]).start(ROW_DMA_PRIORITY)
            return carry
        lax.fori_loop(0, TM_E, body, 0, unroll=ROW_DMA_UNROLL)

    def tile_copy(src, dst, sem):
        return pltpu.make_async_copy(src.at[pl.ds(0, tile_rows)], dst.at[pl.ds(0, tile_rows)], sem)

    @pl.when(i < n_tiles)
    def _():
        @pl.when(c == 0)
        def _():
            @pl.when(i == 0)
            def _():
                gather(0, src0_ref)
                y_buf[...] = jnp.zeros(y_buf.shape, F32)
                spare = pltpu.make_async_copy(y_buf, y_hbm.at[pl.ds(y_hbm.shape[0] - tile_rows, tile_rows)], sem_out)
                spare.start()
                spare.wait()

            @pl.when(i + 1 < n_tiles)
            def _():
                gather((i + 1) % 2, src1_ref)

            buf = i % 2
            tile_copy(h_hbm, h_buf.at[buf], sem_in.at[buf]).wait()
            h_scr[...] = _load_token_rows(h_buf, TM_E, (buf,)).astype(BF16)

        h = h_scr[...]
        g = _dot(h, wg_ref[0].astype(BF16))
        u = _dot(h, wu_ref[0].astype(BF16))
        part = _dot((_silu(g) * u).astype(BF16), wd_ref[0].astype(BF16))

        @pl.when(c == 0)
        def _():
            acc_scr[...] = part

        @pl.when((c > 0) & (c < last_c))
        def _():
            acc_scr[...] += part

        @pl.when(c == last_c)
        def _():
            @pl.when(i > 0)
            def _():
                tile_copy(y_buf, y_hbm, sem_out).wait()
            _store_token_rows(y_buf, acc_scr[...] + part)

            def body(r, carry):
                _token_copy(y_buf, r * TOKEN_ROWS, y_hbm, dst_ref[r], sem_out).start(ROW_DMA_PRIORITY)
                return carry
            lax.fori_loop(0, TM_E, body, 0, unroll=ROW_DMA_UNROLL)

            @pl.when(i == n_tiles - 1)
            def _():
                tile_copy(y_buf, y_hbm, sem_out).wait()


def _experts(tile_expert, n_tiles, slot_src, slot_dst, h_all, wgu, wd, n_out_tokens):
    rows = slot_src.shape[0]
    t = rows // TM_E
    n_c = D_FF_E // FF_CHUNK_E
    assert n_c > 1
    live = lambda i, nt: jnp.maximum(jnp.minimum(i, nt[0] - 1), 0)
    smem_rows = lambda f: pl.BlockSpec((TM_E,), f, memory_space=pltpu.SMEM)
    grid_spec = pltpu.PrefetchScalarGridSpec(
        num_scalar_prefetch=2,
        grid=(t, n_c),
        in_specs=[smem_rows(lambda i, c, te, nt: (0,)),
                  smem_rows(lambda i, c, te, nt: (jnp.minimum(i + 1, t - 1),)),
                  smem_rows(lambda i, c, te, nt: (i,)),
                  pl.BlockSpec(memory_space=pl.ANY),
                  pl.BlockSpec((1, D_MODEL, FF_CHUNK_E), lambda i, c, te, nt: (te[live(i, nt)], 0, c)),
                  pl.BlockSpec((1, D_MODEL, FF_CHUNK_E), lambda i, c, te, nt: (te[live(i, nt)], 0, n_c + c)),
                  pl.BlockSpec((1, FF_CHUNK_E, D_MODEL), lambda i, c, te, nt: (te[live(i, nt)], c, 0))],
        out_specs=pl.BlockSpec(memory_space=pl.ANY),
        scratch_shapes=[pltpu.VMEM((2, TM_E * TOKEN_ROWS, LANES), F32), pltpu.VMEM((TM_E * TOKEN_ROWS, LANES), F32),
                        pltpu.VMEM((TM_E, D_MODEL), BF16), pltpu.VMEM((TM_E, D_MODEL), F32),
                        pltpu.SemaphoreType.DMA((2,)), pltpu.SemaphoreType.DMA(())],
    )
    return pl.pallas_call(
        _experts_kernel,
        out_shape=jax.ShapeDtypeStruct(((n_out_tokens + TM_E) * TOKEN_ROWS, LANES), F32),
        grid_spec=grid_spec,
        compiler_params=pltpu.CompilerParams(dimension_semantics=("arbitrary", "arbitrary"),
                                             vmem_limit_bytes=VMEM_LIMIT, has_side_effects=True),
        name="experts",
    )(tile_expert, n_tiles, slot_src, slot_src, slot_dst, h_all, wgu, wgu, wd)


def _combine_kernel(x_ref, ya_ref, yb_ref, route_ref, mod_ref, g_ref, o_ref):
    w0 = route_ref[:, ROUTE_W0:ROUTE_W0 + 1]
    w1 = route_ref[:, ROUTE_W1:ROUTE_W1 + 1]
    rows = x_ref.shape[0]
    x = x_ref[...] + mod_ref[0, 5:6, :] * (w0 * _load_token_rows(ya_ref, rows) + w1 * _load_token_rows(yb_ref, rows))
    o_ref[...] = x * lax.rsqrt(jnp.mean(x * x, axis=-1, keepdims=True) + EPS) * g_ref[...]


def _combine(x3, y2, route, first_row, n_rows, n_tok, mod_l, cond_of_tile, final_g):
    nt = n_rows // TM
    blk0 = first_row // TM
    blk1 = (n_tok + first_row) // TM
    return pl.pallas_call(
        _combine_kernel,
        out_shape=jax.ShapeDtypeStruct((n_rows, D_MODEL), F32),
        grid=(nt,),
        in_specs=[pl.BlockSpec((TM, D_MODEL), lambda i: (blk0 + i, 0)),
                  pl.BlockSpec((TM * TOKEN_ROWS, LANES), lambda i: (blk0 + i, 0)),
                  pl.BlockSpec((TM * TOKEN_ROWS, LANES), lambda i: (blk1 + i, 0)),
                  pl.BlockSpec((TM, LANES), lambda i: (blk0 + i, 0)),
                  pl.BlockSpec((1, 6, D_MODEL), lambda i: (cond_of_tile(blk0 + i), 0, 0)),
                  _const_spec((1, D_MODEL))],
        out_specs=pl.BlockSpec((TM, D_MODEL), lambda i: (i, 0)),
        compiler_params=_cparams(1),
        name="combine",
    )(x3, y2, y2, route, mod_l, final_g)


def _even_params(ev_w_in, ev_conv_a_w, ev_conv_a_b, ev_a_log, ev_dt_bias, ev_d_skip, ev_gnorm,
                 ev_conv_b_w, ev_conv_b_b, ev_lru_w, ev_lru_b, ev_lru_lam, ev_w_out, ev_ffn_gu, ev_ffn_down):
    d_a = H_A * P_A
    xbc = d_a + 2 * G_A * N_A
    o_xbc, o_dt, o_gate, o_xb = d_a, d_a + xbc, d_a + xbc + 2 * H_A, d_a + xbc + 2 * H_A + D_MODEL
    w = ev_w_in
    w_main = jnp.concatenate([w[:, :d_a], w[:, o_xb:], w[:, o_gate:o_xb], w[:, o_xbc:o_dt]], axis=1).astype(BF16)
    w_dt = jnp.pad(w[:, o_dt:o_gate], ((0, 0), (0, LANES - 2 * H_A))).astype(BF16)
    zeros_w = jnp.zeros((CONV_K, D_MODEL), F32)
    cw = jnp.concatenate([zeros_w, ev_conv_b_w, zeros_w, ev_conv_a_w], axis=1)
    zeros_b = jnp.zeros((D_MODEL,), F32)
    cb = jnp.concatenate([zeros_b, ev_conv_b_b, zeros_b, ev_conv_a_b])[None, :]
    dtb = jnp.pad(ev_dt_bias.reshape(-1), (0, LANES - 2 * H_A))[None, :]
    alog = jnp.pad(ev_a_log.reshape(-1), (0, LANES - 2 * H_A))[None, :]
    j = jnp.arange(2 * LANES) % LANES
    c = jnp.arange(d_a) // P_A
    e2 = jnp.stack([(j[:, None] == (d * H_A + c)[None, :]) for d in range(2)]).astype(BF16)
    wl = jnp.transpose(ev_lru_w, (2, 3, 0, 1, 4)).reshape(H_B, BW_B, 4 * BW_B).astype(BF16)
    lb = jnp.transpose(ev_lru_b.reshape(2, 2, H_B, BW_B), (2, 0, 1, 3)).reshape(H_B, 1, 4 * BW_B)
    lam = jnp.transpose(ev_lru_lam.reshape(2, H_B, BW_B), (1, 0, 2))
    dsk = jnp.repeat(ev_d_skip, P_A)[None, :]
    return dict(w_main=w_main, w_dt=w_dt, cw=cw, cb=cb, dtb=dtb, alog=alog, e2=e2, wl=wl, lb=lb, lam=lam,
                dsk=dsk, gn=ev_gnorm[None, :], w_out=ev_w_out.astype(BF16),
                wgu=ev_ffn_gu.astype(BF16), wd=ev_ffn_down.astype(BF16))


def _rope_partner(w, lo):
    blk = w[:, lo:lo + ROPE_C].reshape(w.shape[0], 2, 2, N_FREQ)
    return jnp.flip(blk, axis=2).reshape(w.shape[0], ROPE_C)


def _odd_params(od_w_in, od_q_norm, od_w_q_up, od_kv_norm, od_w_kv_up, od_w_o, od_router):
    kr_lo = Q_RANK + KV_RANK
    wi = jnp.zeros((D_MODEL, OD_IN_COLS), F32)
    wi = wi.at[:, :kr_lo + ROPE_C].set(od_w_in)
    wi = wi.at[:, 768:768 + ROPE_C].set(_rope_partner(od_w_in, kr_lo))
    wq3 = od_w_q_up.reshape(Q_RANK, H_C, NOPE_C + ROPE_C)
    pad = HEAD_PAD - NOPE_C - ROPE_C
    wq = jnp.pad(wq3, ((0, 0), (0, 0), (0, pad))).reshape(Q_RANK, H_C * HEAD_PAD)
    partner = jnp.flip(wq3[:, :, NOPE_C:].reshape(Q_RANK, H_C, 2, 2, N_FREQ), axis=3).reshape(Q_RANK, H_C, ROPE_C)
    wqs = jnp.pad(partner, ((0, 0), (0, 0), (NOPE_C, pad))).reshape(Q_RANK, H_C * HEAD_PAD)
    wkv3 = od_w_kv_up.reshape(KV_RANK, H_C, NOPE_C + V_C)
    wk = jnp.pad(wkv3[:, :, :NOPE_C], ((0, 0), (0, 0), (0, HEAD_PAD - NOPE_C))).reshape(KV_RANK, H_C * HEAD_PAD)
    wv_t = wkv3[:, :, NOPE_C:].reshape(KV_RANK, H_C * V_C).T
    ek_head = jnp.pad(jnp.eye(ROPE_C, dtype=F32), ((0, 0), (NOPE_C, pad)))
    ek = jnp.tile(ek_head, (1, H_C))
    wr_f = jnp.pad(od_router, ((0, 0), (0, LANES - N_EXPERTS)))
    wr_hi = wr_f.astype(BF16)
    wr = jnp.stack([wr_hi, (wr_f - wr_hi.astype(F32)).astype(BF16)])
    return dict(wi=wi.astype(BF16), qn=od_q_norm[None, :], kvn=od_kv_norm[None, :], wq=wq.astype(BF16),
                wqs=wqs.astype(BF16), wk=wk.astype(BF16), ek=ek.astype(BF16), wv_t=wv_t.astype(BF16),
                wo=od_w_o.astype(BF16), wr=wr)


def _rope_tables(n_tokens):
    rows = n_tokens // GRID_W
    row = jnp.repeat(jnp.arange(rows), GRID_W).astype(F32)
    col = jnp.tile(jnp.arange(GRID_W), rows).astype(F32)
    inv = ROPE_BASE ** (-jnp.arange(N_FREQ, dtype=F32) / N_FREQ)
    ang_r, ang_c = row[:, None] * inv, col[:, None] * inv
    cos_k = jnp.concatenate([jnp.cos(ang_r)] * 2 + [jnp.cos(ang_c)] * 2, axis=1)
    sin_k = jnp.concatenate([-jnp.sin(ang_r), jnp.sin(ang_r), -jnp.sin(ang_c), jnp.sin(ang_c)], axis=1)
    pad = HEAD_PAD - NOPE_C - ROPE_C
    cos_q = jnp.pad(cos_k, ((0, 0), (NOPE_C, pad)), constant_values=1.0)
    sin_q = jnp.pad(sin_k, ((0, 0), (NOPE_C, pad)))
    return cos_q, sin_q, cos_k, sin_k


def _layer0(x2d, mod_l, cond_row, norm_g0, ev, n_batch, seq_len, h0_ssd, h0_lru, is_ctx):
    per_block = max(ROW_BLOCK // seq_len, 1)
    blocks_per_seq = max(seq_len // ROW_BLOCK, 1)
    cond_of_block = (lambda i: 0) if is_ctx else (lambda i: cond_row(i // blocks_per_seq))
    tiles_per_seq = seq_len // TM if seq_len >= TM else 1
    cond_of_tile = (lambda i: 0) if is_ctx else (lambda i: cond_row(i // tiles_per_seq))
    p, dt = _ev_inproj(x2d, mod_l, cond_of_block, norm_g0[0:1], ev["w_main"], ev["w_dt"], ev["cw"], ev["cb"],
                       ev["dtb"], seq_len)
    ssd_out = _ssd(p, dt, ev["alog"], ev["e2"], h0_ssd, n_batch, seq_len, emit_final=is_ctx)
    lru_out = _lru(p, ev["wl"], ev["lb"], ev["lam"], h0_lru, chain=not is_ctx, emit_final=is_ctx)
    x1 = _ev_out(x2d, ssd_out[0], ssd_out[1], p, lru_out[0], ev["w_out"], ev["dsk"], ev["gn"], mod_l, cond_of_tile)
    x2 = _ffn(x1, mod_l, cond_of_tile, norm_g0[1:2], ev["wgu"], ev["wd"])
    finals = None
    if is_ctx:
        hp = H_A * P_A
        s_ssd = ssd_out[2].reshape(n_batch, 1, 2, H_A, P_A, N_A)
        fin = lru_out[1]
        s_lru = jnp.transpose(fin, (0, 3, 2, 1, 4)).reshape(n_batch, 1, 2, H_B * BW_B)
        finals = (s_ssd, s_lru)
        del hp, per_block
    return x2, finals, cond_of_tile


def _layer1_pre(x2d, mod_l, cond_of_tile, norm_g1, od, n_batch, seq_len, rope, cache):
    q, ckv, kr = _od_inproj(x2d, mod_l, cond_of_tile, norm_g1[0:1], od["wi"], od["qn"], od["kvn"], od["wq"],
                            rope, seq_len)
    if cache is None:
        ckv_all, kr_all, kv_len = ckv, kr, seq_len
    else:
        c_ckv, c_kr = cache
        past = c_ckv.shape[1]
        kv_len = past + seq_len
        ckv_all = jnp.concatenate([c_ckv, ckv.reshape(n_batch, seq_len, KV_RANK)], axis=1).reshape(-1, KV_RANK)
        kr_all = jnp.concatenate([c_kr, kr.reshape(n_batch, seq_len, ROPE_C)], axis=1).reshape(-1, ROPE_C)
    k, v_t = _kv_up(ckv_all, kr_all, od["wk"], od["ek"], od["wv_t"])
    o = _attention(q, k, v_t, n_batch, seq_len, kv_len)
    return o, ckv, kr


def kernel(x_prompt, x_sample, state_ssd, state_lru, cache_ckv, cache_krope, c, c_ctx, mod_w, mod_b, norm_g, final_g, ev_w_in, ev_conv_a_w, ev_conv_a_b, ev_a_log, ev_dt_bias, ev_d_skip, ev_gnorm, ev_conv_b_w, ev_conv_b_b, ev_lru_w, ev_lru_b, ev_lru_lam, ev_w_out, ev_ffn_gu, ev_ffn_down, od_w_in, od_q_norm, od_w_q_up, od_kv_norm, od_w_kv_up, od_w_o, od_router, od_moe_gu, od_moe_down):
    batch, seq, _ = x_prompt.shape
    dbatch, dseq, _ = x_sample.shape
    n_ctx, n_dec = batch * seq, dbatch * dseq

    cond = jnp.concatenate([c_ctx[None, :], c, jnp.zeros((16 - 1 - dbatch, D_MODEL), F32)], axis=0)
    mod = _modulation(cond, mod_w, mod_b)
    ev = _even_params(ev_w_in[0], ev_conv_a_w[0], ev_conv_a_b[0], ev_a_log[0], ev_dt_bias[0], ev_d_skip[0],
                      ev_gnorm[0], ev_conv_b_w[0], ev_conv_b_b[0], ev_lru_w[0], ev_lru_b[0], ev_lru_lam[0],
                      ev_w_out[0], ev_ffn_gu[0], ev_ffn_down[0])
    od = _odd_params(od_w_in[0], od_q_norm[0], od_w_q_up[0], od_kv_norm[0], od_w_kv_up[0], od_w_o[0], od_router[0])
    wgu_e = od_moe_gu[0]
    wd_e = od_moe_down[0]
    dec_row = lambda b: 1 + b

    xc, finals, tile_c = _layer0(x_prompt.reshape(n_ctx, D_MODEL), mod[0], None, norm_g[0], ev, batch, seq,
                                 None, None, True)
    h0_ssd = state_ssd[:, 0].reshape(dbatch, 2, H_A * P_A, N_A)
    h0_lru = jnp.transpose(state_lru[:, 0].reshape(dbatch, 2, H_B, BW_B), (0, 2, 1, 3))
    xd, _, tile_d = _layer0(x_sample.reshape(n_dec, D_MODEL), mod[0], dec_row, norm_g[0], ev, dbatch, dseq,
                            h0_ssd, h0_lru, False)
    new_state_ssd, new_state_lru = finals

    rope = (od["wqs"],) + _rope_tables(dseq)
    o_c, ckv_c, kr_c = _layer1_pre(xc, mod[1], tile_c, norm_g[1], od, batch, seq, None, None)
    o_d, _, _ = _layer1_pre(xd, mod[1], tile_d, norm_g[1], od, dbatch, dseq, rope,
                            (cache_ckv[:, 0], cache_krope[:, 0]))
    new_cache_ckv = ckv_c.reshape(batch, 1, seq, KV_RANK)
    new_cache_krope = kr_c.reshape(batch, 1, seq, ROPE_C)

    tiles_c = n_ctx // TM
    tile_all = lambda i: jnp.where(i < tiles_c, 0, tile_d(jnp.maximum(i - tiles_c, 0)))
    x3, h2, route, counts = _od_out_router(o_c, o_d, xc, xd, od["wo"], mod[1], tile_all, norm_g[1][1:2], od["wr"])
    y2 = _routed_experts(h2, route, counts, wgu_e, wd_e)
    n_tok = n_ctx + n_dec
    fg = final_g[None, :]
    y_prompt = _combine(x3, y2, route, 0, n_ctx, n_tok, mod[1], tile_all, fg).reshape(batch, seq, D_MODEL)
    y_sample = _combine(x3, y2, route, n_ctx, n_dec, n_tok, mod[1], tile_all, fg).reshape(dbatch, dseq, D_MODEL)
    return (y_prompt, y_sample, new_state_ssd, new_state_lru, new_cache_ckv, new_cache_krope)


def _routed_experts(h2, route, counts, wgu_e, wd_e):
    n_tok = route.shape[0]
    n_slots = (2 * n_tok // TM_E + N_EXPERTS) * TM_E
    cnt_e = counts[0, :N_EXPERTS].astype(jnp.int32)
    tiles_e = (cnt_e + TM_E - 1) // TM_E
    tile_end = jnp.cumsum(tiles_e)
    slot_start = (tile_end - tiles_e) * TM_E
    n_tiles = tile_end[-1:].astype(jnp.int32)
    tile_id = jnp.arange(n_slots // TM_E, dtype=jnp.int32)
    tile_expert = jnp.minimum(jnp.sum((tile_end[None, :] <= tile_id[:, None]).astype(jnp.int32), axis=1),
                              N_EXPERTS - 1)
    idx = route[:, ROUTE_IDX0:ROUTE_IDX1 + 1].astype(jnp.int32)
    rank = route[:, ROUTE_RANK0:ROUTE_RANK1 + 1].astype(jnp.int32)
    slot = slot_start[idx] + rank
    out_tok = jnp.arange(n_tok, dtype=jnp.int32)[:, None] + jnp.array([0, n_tok], jnp.int32)[None, :]
    spare = 2 * n_tok + jnp.arange(n_slots, dtype=jnp.int32) % TM_E
    slot_dst = spare.at[slot.reshape(-1)].set(out_tok.reshape(-1))
    slot_src = jnp.where(slot_dst >= 2 * n_tok, 0, jnp.where(slot_dst >= n_tok, slot_dst - n_tok, slot_dst))
    return _experts(tile_expert, n_tiles, slot_src * TOKEN_ROWS, slot_dst * TOKEN_ROWS, h2, wgu_e, wd_e, 2 * n_tok)
```

```python
import functools
import math

import jax
import jax.numpy as jnp
from jax import lax
from jax.experimental import pallas as pl
from jax.experimental.pallas import tpu as pltpu

F32 = jnp.float32
BF16 = jnp.bfloat16
HIGHEST = lax.Precision.HIGHEST

D_MODEL = 1024
GRID_W = 64
P_A = 64
H_A = 16
G_A = 2
N_A = 128
CHUNK = 128
CONV_K = 4
H_B = 8
BW_B = 128
LRU_C = 8.0
H_C = 16
Q_RANK = 384
KV_RANK = 256
NOPE_C = 64
ROPE_C = 32
V_C = 64
N_FREQ = ROPE_C // 4
ROPE_BASE = 10000.0
D_FF = 2816
N_EXPERTS = 8
D_FF_E = 3584
EPS = 1e-6

LANES = 128
HEAD_PAD = 128
ROW_BLOCK = 2048
ROW_CHUNK = 256
SEG_PITCH = ROW_CHUNK + 8
TM = 512
TQ_MAX = 256
SSD_CHUNKS_PER_STEP = 4
LOG2_E = 1.4426950408889634
TM_E = 1024
FF_CHUNK_E = 512
VMEM_LIMIT = 52 * 1024 * 1024


def _cparams(n_axes, vmem=VMEM_LIMIT):
    return pltpu.CompilerParams(dimension_semantics=("arbitrary",) * n_axes, vmem_limit_bytes=vmem)


def _dot(a, b):
    return jnp.dot(a, b, preferred_element_type=F32)


def _sigmoid(x):
    return 1.0 / (1.0 + jnp.exp(-x))


def _silu(x):
    return x * _sigmoid(x)


def _softplus(x):
    return jnp.maximum(x, 0.0) + jnp.log(1.0 + jnp.exp(-jnp.abs(x)))


def _gelu_tanh(x):
    return 0.5 * x * (1.0 + jnp.tanh(math.sqrt(2.0 / math.pi) * (x + 0.044715 * (x * x * x))))


def _modulated_norm(x, gain, scale, shift):
    ms = jnp.mean(x * x, axis=-1, keepdims=True)
    return (x * lax.rsqrt(ms + EPS)) * (gain * (1.0 + scale)) + shift


def _const_spec(shape):
    zeros = (0,) * len(shape)
    return pl.BlockSpec(shape, lambda *_: zeros, pipeline_mode=pl.Buffered(1))


def _mod_kernel(c_ref, w_ref, b_ref, o_ref):
    c = c_ref[...]
    o_ref[0] = _dot(_silu(c).astype(BF16), w_ref[0].astype(BF16)) + b_ref[0]


def _modulation(cond, mod_w, mod_b):
    depth = mod_w.shape[0]
    n = 6 * D_MODEL
    tn = n // 4
    out = pl.pallas_call(
        _mod_kernel,
        out_shape=jax.ShapeDtypeStruct((depth, 16, n), F32),
        grid=(depth, 4),
        in_specs=[pl.BlockSpec((16, D_MODEL), lambda l, j: (0, 0)),
                  pl.BlockSpec((1, D_MODEL, tn), lambda l, j: (l, 0, j)),
                  pl.BlockSpec((1, 1, tn), lambda l, j: (l, 0, j))],
        out_specs=pl.BlockSpec((1, 16, tn), lambda l, j: (l, 0, j)),
        compiler_params=_cparams(2),
        name="modulation",
    )(cond, mod_w, mod_b.reshape(depth, 1, n))
    return out.reshape(depth, 16, 6, D_MODEL)


def _ev_inproj_kernel(x_ref, mod_ref, g_ref, w_ref, wdt_ref, cw_ref, cb_ref, dtb_ref,
                      p_ref, dt_ref, h_scr, acc_scr, *, seq_is_chunk):
    j = pl.program_id(1)
    rows = h_scr.shape[0]
    n_chunks = rows // ROW_CHUNK
    width = w_ref.shape[1]

    @pl.when(j == 0)
    def _():
        gain, scale, shift = g_ref[...], mod_ref[0, 1:2, :], mod_ref[0, 0:1, :]

        def body(r, carry):
            r0 = pl.multiple_of(r * ROW_CHUNK, ROW_CHUNK)
            h = _modulated_norm(x_ref[pl.ds(r0, ROW_CHUNK), :], gain, scale, shift)
            h_scr[pl.ds(r0, ROW_CHUNK), :] = h.astype(BF16)
            return carry

        lax.fori_loop(0, n_chunks, body, 0)
        dt = _softplus(_dot(h_scr[...], wdt_ref[...]) + dtb_ref[...])
        lane = lax.broadcasted_iota(jnp.int32, dt.shape, 1)
        dt_ref[...] = jnp.where(lane < 2 * H_A, dt, 0.0)
        acc_scr[0:8, :] = jnp.zeros((8, width), F32)
        acc_scr[rows + 8:rows + 16, :] = jnp.zeros((8, width), F32)

    acc_scr[8:rows + 8, :] = _dot(h_scr[...], w_ref[...])

    def plain(act):
        def body(r, carry):
            r0 = pl.multiple_of(r * ROW_CHUNK, ROW_CHUNK)
            v = acc_scr[pl.ds(r0 + 8, ROW_CHUNK), :]
            p_ref[pl.ds(r0, ROW_CHUNK), :] = act(v).astype(BF16)
            return carry
        lax.fori_loop(0, n_chunks, body, 0)

    def conv(act):
        win = ROW_CHUNK + 16
        cw = cw_ref[...]
        cb = cb_ref[...]

        def body(r, carry):
            r0 = pl.multiple_of(r * ROW_CHUNK, ROW_CHUNK)
            xw = acc_scr[pl.ds(r0, win), :]
            if seq_is_chunk:
                row = lax.broadcasted_iota(jnp.int32, xw.shape, 0)
                xw = jnp.where((row >= 8) & (row < ROW_CHUNK + 8), xw, 0.0)
            y = cb + cw[1:2, :] * xw[8:ROW_CHUNK + 8]
            for k in (0, 2, 3):
                sh = pltpu.roll(xw, (1 - k) % win, 0)[8:ROW_CHUNK + 8]
                y = y + cw[k:k + 1, :] * sh
            p_ref[pl.ds(r0, ROW_CHUNK), :] = act(y).astype(BF16)
            return carry
        lax.fori_loop(0, n_chunks, body, 0)

    @pl.when(j < 2)
    def _():
        plain(lambda v: v)

    @pl.when((j >= 2) & (j < 4))
    def _():
        conv(lambda v: v)

    @pl.when((j >= 4) & (j < 6))
    def _():
        plain(_gelu_tanh)

    @pl.when(j >= 6)
    def _():
        conv(_silu)


P_COLS = 4608


def _ev_inproj(x2d, mod_l, cond_of_block, gain, w, wdt, cw, cb, dtb, seq_len):
    n = x2d.shape[0]
    nblk = n // ROW_BLOCK
    tn = 512
    kern = functools.partial(_ev_inproj_kernel, seq_is_chunk=(seq_len == ROW_CHUNK))
    return pl.pallas_call(
        kern,
        out_shape=(jax.ShapeDtypeStruct((n, P_COLS), BF16), jax.ShapeDtypeStruct((n, LANES), F32)),
        grid=(nblk, P_COLS // tn),
        in_specs=[pl.BlockSpec((ROW_BLOCK, D_MODEL), lambda i, j: (i, 0)),
                  pl.BlockSpec((1, 6, D_MODEL), lambda i, j: (cond_of_block(i), 0, 0)),
                  pl.BlockSpec((1, D_MODEL), lambda i, j: (0, 0)),
                  pl.BlockSpec((D_MODEL, tn), lambda i, j: (0, j)),
                  pl.BlockSpec((D_MODEL, LANES), lambda i, j: (0, 0)),
                  pl.BlockSpec((CONV_K, tn), lambda i, j: (0, j)),
                  pl.BlockSpec((1, tn), lambda i, j: (0, j)),
                  pl.BlockSpec((1, LANES), lambda i, j: (0, 0))],
        out_specs=(pl.BlockSpec((ROW_BLOCK, tn), lambda i, j: (i, j)),
                   pl.BlockSpec((ROW_BLOCK, LANES), lambda i, j: (i, 0))),
        scratch_shapes=[pltpu.VMEM((ROW_BLOCK, D_MODEL), BF16),
                        pltpu.VMEM((ROW_BLOCK + 16, tn), F32)],
        compiler_params=_cparams(2),
        name="ev_inproj",
    )(x2d, mod_l, gain, w, wdt, cw, cb, dtb)


def _ssd_direction(d, xs_ref, bm_ref, cm_ref, dt_ref, arate, e2, h_scr, y_ref):
    xs = xs_ref[...].astype(F32)
    bm = bm_ref[...]
    cm = cm_ref[...]
    a = dt_ref[...] * arate
    li = lax.broadcasted_iota(jnp.int32, (CHUNK, CHUNK), 0)
    si = lax.broadcasted_iota(jnp.int32, (CHUNK, CHUNK), 1)
    keep = (si <= li) if d == 0 else (si >= li)
    tri = jnp.where(keep, 1.0, 0.0).astype(F32)
    acs = jnp.dot(tri, a, precision=HIGHEST, preferred_element_type=F32)
    acs_t = acs.T
    tot = acs[CHUNK - 1:CHUNK, :] if d == 0 else acs[0:1, :]
    stack = jnp.concatenate([dt_ref[...], jnp.exp(acs), jnp.exp(tot - acs)], axis=0)
    hi = stack.astype(BF16)
    lo = (stack - hi.astype(F32)).astype(BF16)
    ex = _dot(jnp.concatenate([hi, lo], axis=1), e2)
    dt_e, eacs_e, ds_e = ex[0:CHUNK], ex[CHUNK:2 * CHUNK], ex[2 * CHUNK:3 * CHUNK]
    xdt = xs * dt_e
    xdt_b = xdt.astype(BF16)
    xds_b = (xdt * ds_e).astype(BF16)
    lane = lax.broadcasted_iota(jnp.int32, (CHUNK, LANES), 1)
    gw = (H_A // G_A) * P_A
    y_groups = []
    for g in range(G_A):
        bm_g = bm[:, g * N_A:(g + 1) * N_A]
        cm_g = cm[:, g * N_A:(g + 1) * N_A]
        cb = lax.dot_general(cm_g, bm_g, (((1,), (1,)), ((), ())), preferred_element_type=F32)
        h_prev = h_scr[d, :, g * gw:(g + 1) * gw]
        y_off = _dot(cm_g, h_prev.astype(BF16)) * eacs_e[:, g * gw:(g + 1) * gw]
        bm_t = bm_g.astype(F32).T.astype(BF16)
        st = _dot(bm_t, xds_b[:, g * gw:(g + 1) * gw])
        decay_tot = eacs_e[CHUNK - 1:CHUNK, g * gw:(g + 1) * gw] if d == 0 else eacs_e[0:1, g * gw:(g + 1) * gw]
        h_scr[d, :, g * gw:(g + 1) * gw] = decay_tot * h_prev + st
        pairs = []
        for q in range(H_A // G_A // 2):
            ms = []
            for e in (g * 8 + 2 * q, g * 8 + 2 * q + 1):
                col = acs[:, d * H_A + e:d * H_A + e + 1]
                row = acs_t[d * H_A + e:d * H_A + e + 1, :]
                dec = jnp.where(keep, jnp.exp(jnp.minimum(col - row, 0.0)), 0.0)
                ms.append((cb * dec).astype(BF16))
            lhs = jnp.concatenate(ms, axis=1)
            xp = xdt_b[:, (g * 8 + 2 * q) * P_A:(g * 8 + 2 * q + 2) * P_A]
            zero = jnp.zeros_like(xp)
            rhs = jnp.concatenate([jnp.where(lane < P_A, xp, zero), jnp.where(lane >= P_A, xp, zero)], axis=0)
            pairs.append(_dot(lhs, rhs))
        y_groups.append(jnp.concatenate(pairs, axis=1) + y_off)
    y_ref[...] = jnp.concatenate(y_groups, axis=1).astype(y_ref.dtype)


def _ssd_kernel(*refs, has_h0, emit_final):
    (xsf, bmf, cmf, dtf, xsb, bmb, cmb, dtb, alog_ref, e2_ref), rest = refs[:10], refs[10:]
    if has_h0:
        h0_ref, rest = rest[0], rest[1:]
    yf_ref, yb_ref, rest = rest[0], rest[1], rest[2:]
    if emit_final:
        hfin_ref, rest = rest[0], rest[1:]
    h_scr = rest[0]
    i = pl.program_id(1)

    @pl.when(i == 0)
    def _():
        for d in range(2):
            if has_h0:
                h_scr[d] = h0_ref[0, d].T
            else:
                h_scr[d] = jnp.zeros(h_scr.shape[1:], F32)

    arate = -jnp.exp(alog_ref[...])
    chunks_per_step = xsf.shape[0] // CHUNK
    for s in range(chunks_per_step):
        sf = pl.ds(s * CHUNK, CHUNK)
        sb = pl.ds((chunks_per_step - 1 - s) * CHUNK, CHUNK)
        _ssd_direction(0, xsf.at[sf], bmf.at[sf], cmf.at[sf], dtf.at[sf], arate, e2_ref[0], h_scr, yf_ref.at[sf])
        _ssd_direction(1, xsb.at[sb], bmb.at[sb], cmb.at[sb], dtb.at[sb], arate, e2_ref[1], h_scr, yb_ref.at[sb])

    if emit_final:
        @pl.when(i == pl.num_programs(1) - 1)
        def _():
            for d in range(2):
                hfin_ref[0, d] = h_scr[d].T


def _ssd(p, dt, alog_row, e2, h0, n_batch, seq_len, emit_final):
    n = p.shape[0]
    rows = min(SSD_CHUNKS_PER_STEP * CHUNK, seq_len)
    nc = seq_len // rows
    hp = H_A * P_A
    fwd = lambda b, i: b * nc + i
    bwd = lambda b, i: b * nc + (nc - 1 - i)
    xs_blk, bm_blk, cm_blk = 3, 16, 17

    def specs(rowf):
        return [pl.BlockSpec((rows, hp), lambda b, i: (rowf(b, i), xs_blk)),
                pl.BlockSpec((rows, G_A * N_A), lambda b, i: (rowf(b, i), bm_blk)),
                pl.BlockSpec((rows, G_A * N_A), lambda b, i: (rowf(b, i), cm_blk)),
                pl.BlockSpec((rows, LANES), lambda b, i: (rowf(b, i), 0))]

    in_specs = specs(fwd) + specs(bwd) + [pl.BlockSpec((1, LANES), lambda b, i: (0, 0)),
                                          pl.BlockSpec((2, 2 * LANES, hp), lambda b, i: (0, 0, 0))]
    args = [p, p, p, dt, p, p, p, dt, alog_row, e2]
    if h0 is not None:
        in_specs.append(pl.BlockSpec((1, 2, hp, N_A), lambda b, i: (b, 0, 0, 0)))
        args.append(h0)
    out_shape = [jax.ShapeDtypeStruct((n, hp), BF16), jax.ShapeDtypeStruct((n, hp), BF16)]
    out_specs = [pl.BlockSpec((rows, hp), lambda b, i: (fwd(b, i), 0)),
                 pl.BlockSpec((rows, hp), lambda b, i: (bwd(b, i), 0))]
    if emit_final:
        out_shape.append(jax.ShapeDtypeStruct((n_batch, 2, hp, N_A), F32))
        out_specs.append(pl.BlockSpec((1, 2, hp, N_A), lambda b, i: (b, 0, 0, 0)))
    kern = functools.partial(_ssd_kernel, has_h0=h0 is not None, emit_final=emit_final)
    return pl.pallas_call(
        kern, out_shape=tuple(out_shape), grid=(n_batch, nc),
        in_specs=in_specs, out_specs=tuple(out_specs),
        scratch_shapes=[pltpu.VMEM((2, N_A, hp), F32)],
        compiler_params=_cparams(2),
        name="ssd",
    )(*args)


def _lru_kernel(*refs, chain, emit_final):
    (xr_ref, gg_ref, wl_ref, lb_ref, lam_ref), rest = refs[:5], refs[5:]
    if chain:
        h0_ref, rest = rest[0], rest[1:]
    y_ref, rest = rest[0], rest[1:]
    if emit_final:
        fin_ref, rest = rest[0], rest[1:]
    a_scr, u_scr, h_scr, p_scr = rest
    rows = xr_ref.shape[0]
    n_seg = rows // ROW_CHUNK
    sp = _softplus(-lam_ref[0])

    def gates(r, carry):
        r0 = pl.multiple_of(r * ROW_CHUNK, ROW_CHUNK)
        s0 = pl.multiple_of(r * SEG_PITCH, 8)
        xr_b = xr_ref[pl.ds(r0, ROW_CHUNK), :]
        xr = xr_b.astype(F32)
        g = _dot(xr_b, wl_ref[0]) + lb_ref[0]
        for d in range(2):
            r_gate = _sigmoid(g[:, (2 * d) * BW_B:(2 * d + 1) * BW_B])
            i_gate = _sigmoid(g[:, (2 * d + 1) * BW_B:(2 * d + 2) * BW_B])
            a = jnp.exp((-LRU_C) * r_gate * sp[d:d + 1, :])
            u = jnp.sqrt(jnp.maximum(1.0 - a * a, 0.0)) * (i_gate * xr)
            a_scr[d, pl.ds(s0, ROW_CHUNK), :] = a
            u_scr[d, pl.ds(s0, ROW_CHUNK), :] = u
        return carry

    lax.fori_loop(0, n_seg, gates, 0, unroll=4)

    def step(t, carry):
        hf, hb, pf, pb = carry
        tb = ROW_CHUNK - 1 - t
        sf = pl.ds(t, n_seg, stride=SEG_PITCH)
        sb = pl.ds(tb, n_seg, stride=SEG_PITCH)
        af = a_scr[0, sf, :]
        ab = a_scr[1, sb, :]
        hf = af * hf + u_scr[0, sf, :]
        hb = ab * hb + u_scr[1, sb, :]
        h_scr[0, sf, :] = hf
        h_scr[1, sb, :] = hb
        if chain:
            pf = af * pf
            pb = ab * pb
            p_scr[0, sf, :] = pf
            p_scr[1, sb, :] = pb
        return hf, hb, pf, pb

    z = jnp.zeros((n_seg, BW_B), F32)
    o = jnp.ones((n_seg, BW_B), F32)
    lax.fori_loop(0, ROW_CHUNK, step, (z, z, o, o), unroll=8)

    if emit_final:
        fin_ref[0, 0, 0] = h_scr[0, pl.ds(ROW_CHUNK - 1, n_seg, stride=SEG_PITCH), :]
        fin_ref[0, 0, 1] = h_scr[1, pl.ds(0, n_seg, stride=SEG_PITCH), :]

    if chain:
        carry = h0_ref[0, 0, 0:1, :]
        for s in range(n_seg):
            sl = slice(s * SEG_PITCH, s * SEG_PITCH + ROW_CHUNK)
            h = h_scr[0, sl, :] + p_scr[0, sl, :] * carry
            h_scr[0, sl, :] = h
            carry = h[ROW_CHUNK - 1:ROW_CHUNK, :]
        carry = h0_ref[0, 0, 1:2, :]
        for s in reversed(range(n_seg)):
            sl = slice(s * SEG_PITCH, s * SEG_PITCH + ROW_CHUNK)
            h = h_scr[1, sl, :] + p_scr[1, sl, :] * carry
            h_scr[1, sl, :] = h
            carry = h[0:1, :]

    for s in range(n_seg):
        sl = slice(s * SEG_PITCH, s * SEG_PITCH + ROW_CHUNK)
        gg = gg_ref[s * ROW_CHUNK:(s + 1) * ROW_CHUNK, :].astype(F32)
        y_ref[s * ROW_CHUNK:(s + 1) * ROW_CHUNK, :] = (gg * (h_scr[0, sl, :] + h_scr[1, sl, :])).astype(BF16)


def _lru(p, wl, lb, lam, h0, chain, emit_final):
    n = p.shape[0]
    nblk = n // ROW_BLOCK
    n_seg = ROW_BLOCK // ROW_CHUNK
    xb_blk, gate_blk = D_MODEL // BW_B, 2 * D_MODEL // BW_B
    in_specs = [pl.BlockSpec((ROW_BLOCK, BW_B), lambda i, h: (i, xb_blk + h)),
                pl.BlockSpec((ROW_BLOCK, BW_B), lambda i, h: (i, gate_blk + h)),
                pl.BlockSpec((1, BW_B, 4 * BW_B), lambda i, h: (h, 0, 0)),
                pl.BlockSpec((1, 1, 4 * BW_B), lambda i, h: (h, 0, 0)),
                pl.BlockSpec((1, 2, BW_B), lambda i, h: (h, 0, 0))]
    args = [p, p, wl, lb, lam]
    if chain:
        in_specs.append(pl.BlockSpec((1, 1, 2, BW_B), lambda i, h: (i, h, 0, 0)))
        args.append(h0)
    out_shape = [jax.ShapeDtypeStruct((n, D_MODEL), BF16)]
    out_specs = [pl.BlockSpec((ROW_BLOCK, BW_B), lambda i, h: (i, h))]
    if emit_final:
        out_shape.append(jax.ShapeDtypeStruct((nblk, H_B, 2, n_seg, BW_B), F32))
        out_specs.append(pl.BlockSpec((1, 1, 2, n_seg, BW_B), lambda i, h: (i, h, 0, 0, 0)))
    seg_rows = n_seg * SEG_PITCH
    kern = functools.partial(_lru_kernel, chain=chain, emit_final=emit_final)
    return pl.pallas_call(
        kern, out_shape=tuple(out_shape), grid=(nblk, H_B),
        in_specs=in_specs, out_specs=tuple(out_specs),
        scratch_shapes=[pltpu.VMEM((2, seg_rows, BW_B), F32)] * 4,
        compiler_params=_cparams(2),
        name="lru",
    )(*args)


def _ev_out_kernel(x_ref, yf_ref, yb_ref, xs_ref, z_ref, yl_ref, wo_ref, dsk_ref, gn_ref, mod_ref, o_ref):
    y = yf_ref[...].astype(F32) + yb_ref[...].astype(F32) + dsk_ref[...] * xs_ref[...].astype(F32)
    y = y * _silu(z_ref[...].astype(F32))
    gw = D_MODEL // G_A
    parts = []
    for g in range(G_A):
        yg = y[:, g * gw:(g + 1) * gw]
        ms = jnp.mean(yg * yg, axis=-1, keepdims=True)
        parts.append((yg * lax.rsqrt(ms + EPS) * gn_ref[:, g * gw:(g + 1) * gw]).astype(BF16))
    mix = _dot(parts[0], wo_ref[0:gw, :]) + _dot(parts[1], wo_ref[gw:2 * gw, :])
    mix = mix + _dot(yl_ref[...], wo_ref[D_MODEL:2 * D_MODEL, :])
    o_ref[...] = x_ref[...] + mod_ref[0, 2:3, :] * mix


def _ev_out(x2d, yf, yb, p, ylru, wo, dsk, gn, mod_l, cond_of_tile):
    n = x2d.shape[0]
    row = lambda i: (i, 0)
    return pl.pallas_call(
        _ev_out_kernel,
        out_shape=jax.ShapeDtypeStruct((n, D_MODEL), F32),
        grid=(n // TM,),
        in_specs=[pl.BlockSpec((TM, D_MODEL), row), pl.BlockSpec((TM, D_MODEL), row),
                  pl.BlockSpec((TM, D_MODEL), row),
                  pl.BlockSpec((TM, D_MODEL), lambda i: (i, 3)),
                  pl.BlockSpec((TM, D_MODEL), lambda i: (i, 0)),
                  pl.BlockSpec((TM, D_MODEL), row),
                  _const_spec((2 * D_MODEL, D_MODEL)), _const_spec((1, D_MODEL)), _const_spec((1, D_MODEL)),
                  pl.BlockSpec((1, 6, D_MODEL), lambda i: (cond_of_tile(i), 0, 0))],
        out_specs=pl.BlockSpec((TM, D_MODEL), row),
        compiler_params=_cparams(1),
        name="ev_out",
    )(x2d, yf, yb, p, p, ylru, wo, dsk, gn, mod_l)


def _ffn_kernel(x_ref, mod_ref, g_ref, wgu_ref, wd_ref, o_ref):
    x = x_ref[...]
    h = _modulated_norm(x, g_ref[...], mod_ref[0, 4:5, :], mod_ref[0, 3:4, :]).astype(BF16)
    n_chunk = 2
    fc = D_FF // n_chunk
    f = None
    for c in range(n_chunk):
        g = _dot(h, wgu_ref[:, c * fc:(c + 1) * fc])
        u = _dot(h, wgu_ref[:, D_FF + c * fc:D_FF + (c + 1) * fc])
        part = _dot((_silu(g) * u).astype(BF16), wd_ref[c * fc:(c + 1) * fc, :])
        f = part if f is None else f + part
    o_ref[...] = x + mod_ref[0, 5:6, :] * f


def _ffn(x2d, mod_l, cond_of_tile, gain, wgu, wd):
    n = x2d.shape[0]
    return pl.pallas_call(
        _ffn_kernel,
        out_shape=jax.ShapeDtypeStruct((n, D_MODEL), F32),
        grid=(n // TM,),
        in_specs=[pl.BlockSpec((TM, D_MODEL), lambda i: (i, 0)),
                  pl.BlockSpec((1, 6, D_MODEL), lambda i: (cond_of_tile(i), 0, 0)),
                  _const_spec((1, D_MODEL)), _const_spec((D_MODEL, 2 * D_FF)), _const_spec((D_FF, D_MODEL))],
        out_specs=pl.BlockSpec((TM, D_MODEL), lambda i: (i, 0)),
        compiler_params=_cparams(1),
        name="ffn",
    )(x2d, mod_l, gain, wgu, wd)


OD_IN_COLS = 896


def _od_inproj_kernel(*refs, use_rope):
    (x_ref, mod_ref, g_ref, wi_ref, qn_ref, kvn_ref, wq_ref), rest = refs[:7], refs[7:]
    if use_rope:
        (wqs_ref, cq_ref, sq_ref, ck_ref, sk_ref), rest = rest[:5], rest[5:]
    q_ref, ckv_ref, kr_ref = rest
    h = _modulated_norm(x_ref[...], g_ref[...], mod_ref[0, 1:2, :], mod_ref[0, 0:1, :]).astype(BF16)
    proj = _dot(h, wi_ref[...])
    cq = proj[:, 0:Q_RANK]
    cqn = (cq * lax.rsqrt(jnp.mean(cq * cq, axis=-1, keepdims=True) + EPS) * qn_ref[...]).astype(BF16)
    ckv = proj[:, Q_RANK:Q_RANK + KV_RANK]
    ckv_ref[...] = ckv * lax.rsqrt(jnp.mean(ckv * ckv, axis=-1, keepdims=True) + EPS) * kvn_ref[...]
    kr = proj[:, 640:640 + ROPE_C]
    q = _dot(cqn, wq_ref[...])
    if use_rope:
        q = q * jnp.tile(cq_ref[...], (1, H_C)) + _dot(cqn, wqs_ref[...]) * jnp.tile(sq_ref[...], (1, H_C))
        kr = kr * ck_ref[...] + proj[:, 768:768 + ROPE_C] * sk_ref[...]
    q = (q * ((NOPE_C + ROPE_C) ** -0.5 * LOG2_E)).astype(BF16)
    for h in range(H_C):
        q_ref[h] = q[:, h * HEAD_PAD:(h + 1) * HEAD_PAD]
    kr_ref[...] = kr


def _od_inproj(x2d, mod_l, cond_of_tile, gain, wi, qn, kvn, wq, rope, seq_len):
    n = x2d.shape[0]
    use_rope = rope is not None
    in_specs = [pl.BlockSpec((TM, D_MODEL), lambda i: (i, 0)),
                pl.BlockSpec((1, 6, D_MODEL), lambda i: (cond_of_tile(i), 0, 0)),
                _const_spec((1, D_MODEL)), _const_spec((D_MODEL, OD_IN_COLS)),
                _const_spec((1, Q_RANK)), _const_spec((1, KV_RANK)),
                _const_spec((Q_RANK, H_C * HEAD_PAD))]
    args = [x2d, mod_l, gain, wi, qn, kvn, wq]
    if use_rope:
        wqs, cos_q, sin_q, cos_k, sin_k = rope
        per_seq = seq_len // TM
        pos = lambda i: (i % per_seq, 0)
        in_specs += [_const_spec((Q_RANK, H_C * HEAD_PAD)),
                     pl.BlockSpec((TM, HEAD_PAD), pos), pl.BlockSpec((TM, HEAD_PAD), pos),
                     pl.BlockSpec((TM, ROPE_C), pos), pl.BlockSpec((TM, ROPE_C), pos)]
        args += [wqs, cos_q, sin_q, cos_k, sin_k]
    return pl.pallas_call(
        functools.partial(_od_inproj_kernel, use_rope=use_rope),
        out_shape=(jax.ShapeDtypeStruct((H_C, n, HEAD_PAD), BF16),
                   jax.ShapeDtypeStruct((n, KV_RANK), F32),
                   jax.ShapeDtypeStruct((n, ROPE_C), F32)),
        grid=(n // TM,),
        in_specs=in_specs,
        out_specs=(pl.BlockSpec((H_C, TM, HEAD_PAD), lambda i: (0, i, 0)),
                   pl.BlockSpec((TM, KV_RANK), lambda i: (i, 0)),
                   pl.BlockSpec((TM, ROPE_C), lambda i: (i, 0))),
        compiler_params=_cparams(1),
        name="od_inproj",
    )(*args)


def _kv_up_kernel(ckv_ref, kr_ref, wk_ref, ek_ref, wv_ref, k_ref, v_ref):
    ckv = ckv_ref[...].astype(BF16)
    k = (_dot(ckv, wk_ref[...]) + _dot(kr_ref[...].astype(BF16), ek_ref[...])).astype(BF16)
    v_t = lax.dot_general(wv_ref[...], ckv, (((1,), (1,)), ((), ())), preferred_element_type=F32)
    for h in range(H_C):
        k_ref[h] = k[:, h * HEAD_PAD:(h + 1) * HEAD_PAD]
    v_ref[...] = v_t.astype(BF16)


def _kv_up(ckv_all, kr_all, wk, ek, wv_t):
    m = ckv_all.shape[0]
    return pl.pallas_call(
        _kv_up_kernel,
        out_shape=(jax.ShapeDtypeStruct((H_C, m, HEAD_PAD), BF16),
                   jax.ShapeDtypeStruct((H_C * V_C, m), BF16)),
        grid=(m // TM,),
        in_specs=[pl.BlockSpec((TM, KV_RANK), lambda i: (i, 0)), pl.BlockSpec((TM, ROPE_C), lambda i: (i, 0)),
                  _const_spec((KV_RANK, H_C * HEAD_PAD)), _const_spec((ROPE_C, H_C * HEAD_PAD)),
                  _const_spec((H_C * V_C, KV_RANK))],
        out_specs=(pl.BlockSpec((H_C, TM, HEAD_PAD), lambda i: (0, i, 0)),
                   pl.BlockSpec((H_C * V_C, TM), lambda i: (0, i))),
        compiler_params=_cparams(1),
        name="kv_up",
    )(ckv_all, kr_all, wk, ek, wv_t)


def _attn_kernel(q_ref, k_ref, v_ref, o_ref, sa_ref, sb_ref):
    def scores(h, s_ref):
        s_ref[...] = lax.dot_general(k_ref[h], q_ref[h], (((1,), (1,)), ((), ())), preferred_element_type=F32)

    def over_keys(x, reduce):
        slabs = x.reshape(x.shape[0] // LANES, LANES, x.shape[1])
        return reduce(reduce(slabs, axis=0), axis=0, keepdims=True)

    def weighted_values(h, s_ref):
        s_t = s_ref[...]
        p_t = jnp.exp2(s_t - over_keys(s_t, jnp.max))
        l = over_keys(p_t, jnp.sum)
        v_t = v_ref[h * V_C:(h + 1) * V_C, :]
        return _dot(v_t, p_t.astype(BF16)) / l

    scores(0, sa_ref)
    for hp in range(H_C // 2):
        h = 2 * hp
        scores(h + 1, sb_ref)
        o_a = weighted_values(h, sa_ref)
        if h + 2 < H_C:
            scores(h + 2, sa_ref)
        o_b = weighted_values(h + 1, sb_ref)
        o_ref[hp] = jnp.concatenate([o_a, o_b], axis=0).T.astype(BF16)


def _attention(q, k, v_t, n_batch, seq_len, kv_len):
    tq = min(seq_len, TQ_MAX)
    per = seq_len // tq
    return pl.pallas_call(
        _attn_kernel,
        out_shape=jax.ShapeDtypeStruct((H_C // 2, q.shape[1], LANES), BF16),
        grid=(n_batch, per),
        in_specs=[pl.BlockSpec((H_C, tq, HEAD_PAD), lambda b, i: (0, b * per + i, 0)),
                  pl.BlockSpec((H_C, kv_len, HEAD_PAD), lambda b, i: (0, b, 0)),
                  pl.BlockSpec((H_C * V_C, kv_len), lambda b, i: (0, b))],
        out_specs=pl.BlockSpec((H_C // 2, tq, LANES), lambda b, i: (0, b * per + i, 0)),
        scratch_shapes=[pltpu.VMEM((kv_len, tq), F32)] * 2,
        compiler_params=_cparams(2),
        name="attention",
    )(q, k, v_t)


TOKEN_ROWS = D_MODEL // LANES


def _store_token_rows(ref, value, index=()):
    for s in range(TOKEN_ROWS):
        ref[index + (pl.ds(s, value.shape[0], stride=TOKEN_ROWS), slice(None))] = value[:, s * LANES:(s + 1) * LANES]


def _load_token_rows(ref, rows, index=()):
    parts = [ref[index + (pl.ds(s, rows, stride=TOKEN_ROWS), slice(None))] for s in range(TOKEN_ROWS)]
    return jnp.concatenate(parts, axis=-1)


ROUTE_IDX0, ROUTE_IDX1, ROUTE_RANK0, ROUTE_RANK1, ROUTE_W0, ROUTE_W1 = range(6)
ROUTER_SPLIT = 2


def _od_out_router_kernel(oa_ref, ob_ref, xa_ref, xb_ref, wo_ref, mod_ref, g_ref, wr_ref,
                          x3_ref, h2_ref, route_ref, cnt_ref, cnt_scr, *, tiles_a):
    i = pl.program_id(0)

    @pl.when(i == 0)
    def _():
        cnt_scr[...] = jnp.zeros(cnt_scr.shape, F32)

    from_a = i < tiles_a
    w_hi, w_lo = wr_ref[0], wr_ref[1]
    tm = x3_ref.shape[0] // ROUTER_SPLIT
    lane = lax.broadcasted_iota(jnp.int32, (tm, LANES), 1)
    lane_f = lane.astype(F32)
    neg = jnp.float32(-jnp.inf)
    ri = lax.broadcasted_iota(jnp.int32, (tm, tm), 0)
    ci = lax.broadcasted_iota(jnp.int32, (tm, tm), 1)
    before = jnp.where(ci < ri, 1.0, 0.0).astype(BF16)

    def rows_part(r0, seen):
        rs = pl.ds(r0, tm)
        o = jnp.concatenate([jnp.where(from_a, oa_ref[hp, rs, :], ob_ref[hp, rs, :]) for hp in range(H_C // 2)],
                            axis=-1)
        x3 = jnp.where(from_a, xa_ref[rs, :], xb_ref[rs, :]) + mod_ref[0, 2:3, :] * _dot(o, wo_ref[...])
        x3_ref[rs, :] = x3
        h2 = _modulated_norm(x3, g_ref[...], mod_ref[0, 4:5, :], mod_ref[0, 3:4, :])
        _store_token_rows(h2_ref.at[pl.ds(r0 * TOKEN_ROWS, tm * TOKEN_ROWS)], h2)
        h_hi = h2.astype(BF16)
        h_lo = (h2 - h_hi.astype(F32)).astype(BF16)
        logits = _dot(h_hi, w_hi) + (_dot(h_lo, w_hi) + _dot(h_hi, w_lo))
        logits = jnp.where(lane < N_EXPERTS, logits, neg)
        m0 = jnp.max(logits, axis=-1, keepdims=True)
        i0 = jnp.min(jnp.where(logits == m0, lane_f, float(LANES)), axis=-1, keepdims=True)
        rest = jnp.where(lane_f == i0, neg, logits)
        m1 = jnp.max(rest, axis=-1, keepdims=True)
        i1 = jnp.min(jnp.where(rest == m1, lane_f, float(LANES)), axis=-1, keepdims=True)
        e = jnp.exp(m1 - m0)
        w0 = 1.0 / (1.0 + e)
        w1 = e * w0
        oh0 = jnp.where(lane_f == i0, 1.0, 0.0).astype(F32)
        oh1 = jnp.where(lane_f == i1, 1.0, 0.0).astype(F32)
        sel = oh0 + oh1
        prior = _dot(before, sel.astype(BF16)) + seen
        rank0 = jnp.sum(oh0 * prior, axis=-1, keepdims=True)
        rank1 = jnp.sum(oh1 * prior, axis=-1, keepdims=True)
        route = jnp.where(lane == ROUTE_IDX0, i0, 0.0)
        route = jnp.where(lane == ROUTE_IDX1, i1, route)
        route = jnp.where(lane == ROUTE_RANK0, rank0, route)
        route = jnp.where(lane == ROUTE_RANK1, rank1, route)
        route = jnp.where(lane == ROUTE_W0, w0, route)
        route = jnp.where(lane == ROUTE_W1, w1, route)
        route_ref[rs, :] = route
        return seen + jnp.sum(sel, axis=0, keepdims=True)

    seen = cnt_scr[0:1, :]
    for part in range(ROUTER_SPLIT):
        seen = rows_part(part * tm, seen)
    cnt_scr[...] = jnp.broadcast_to(seen, cnt_scr.shape)
    cnt_ref[...] = cnt_scr[...]


def _od_out_router(o_a, o_b, x_a, x_b, wo, mod_l, cond_of_tile, gain, wr):
    n_a, n_b = x_a.shape[0], x_b.shape[0]
    n = n_a + n_b
    tiles_a = n_a // TM
    row = lambda i: (i, 0)
    a_tile = lambda i: jnp.minimum(i, tiles_a - 1)
    b_tile = lambda i: jnp.maximum(i - tiles_a, 0)
    return pl.pallas_call(
        functools.partial(_od_out_router_kernel, tiles_a=tiles_a),
        out_shape=(jax.ShapeDtypeStruct((n, D_MODEL), F32), jax.ShapeDtypeStruct((n * TOKEN_ROWS, LANES), F32),
                   jax.ShapeDtypeStruct((n, LANES), F32), jax.ShapeDtypeStruct((8, LANES), F32)),
        grid=(n // TM,),
        in_specs=[pl.BlockSpec((H_C // 2, TM, LANES), lambda i: (0, a_tile(i), 0)),
                  pl.BlockSpec((H_C // 2, TM, LANES), lambda i: (0, b_tile(i), 0)),
                  pl.BlockSpec((TM, D_MODEL), lambda i: (a_tile(i), 0)),
                  pl.BlockSpec((TM, D_MODEL), lambda i: (b_tile(i), 0)),
                  _const_spec((D_MODEL, D_MODEL)),
                  pl.BlockSpec((1, 6, D_MODEL), lambda i: (cond_of_tile(i), 0, 0)),
                  _const_spec((1, D_MODEL)), _const_spec((2, D_MODEL, LANES))],
        out_specs=(pl.BlockSpec((TM, D_MODEL), row), pl.BlockSpec((TM * TOKEN_ROWS, LANES), row),
                   pl.BlockSpec((TM, LANES), row), pl.BlockSpec((8, LANES), lambda i: (0, 0))),
        scratch_shapes=[pltpu.VMEM((8, LANES), F32)],
        compiler_params=_cparams(1),
        name="od_out_router",
    )(o_a, o_b, x_a, x_b, wo, mod_l, gain, wr)


def _token_copy(src, s8, dst, d8, sem):
    return pltpu.make_async_copy(src.at[pl.ds(pl.multiple_of(s8, TOKEN_ROWS), TOKEN_ROWS)],
                                 dst.at[pl.ds(pl.multiple_of(d8, TOKEN_ROWS), TOKEN_ROWS)], sem)


ROW_DMA_UNROLL = 8
ROW_DMA_PRIORITY = 1


def _experts_kernel(te_ref, nt_ref, src0_ref, src1_ref, dst_ref, h_hbm, wg_ref, wu_ref, wd_ref, y_hbm,
                    h_buf, y_buf, h_scr, acc_scr, sem_in, sem_out):
    i, c = pl.program_id(0), pl.program_id(1)
    last_c = pl.num_programs(1) - 1
    n_tiles = nt_ref[0]
    tile_rows = TM_E * TOKEN_ROWS

    def gather(buf, src_ref):
        def body(r, carry):
            _token_copy(h_hbm, src_ref[r], h_buf.at[buf], r * TOKEN_ROWS, sem_in.at[buf]).start(ROW_DMA_PRIORITY)
            return carry
        lax.fori_loop(0, TM_E, body, 0, unroll=ROW_DMA_UNROLL)

    def tile_copy(src, dst, sem):
        return pltpu.make_async_copy(src.at[pl.ds(0, tile_rows)], dst.at[pl.ds(0, tile_rows)], sem)

    @pl.when(i < n_tiles)
    def _():
        @pl.when(c == 0)
        def _():
            @pl.when(i == 0)
            def _():
                gather(0, src0_ref)
                y_buf[...] = jnp.zeros(y_buf.shape, F32)
                spare = pltpu.make_async_copy(y_buf, y_hbm.at[pl.ds(y_hbm.shape[0] - tile_rows, tile_rows)], sem_out)
                spare.start()
                spare.wait()

            @pl.when(i + 1 < n_tiles)
            def _():
                gather((i + 1) % 2, src1_ref)

            buf = i % 2
            tile_copy(h_hbm, h_buf.at[buf], sem_in.at[buf]).wait()
            h_scr[...] = _load_token_rows(h_buf, TM_E, (buf,)).astype(BF16)

        h = h_scr[...]
        g = _dot(h, wg_ref[0].astype(BF16))
        u = _dot(h, wu_ref[0].astype(BF16))
        part = _dot((_silu(g) * u).astype(BF16), wd_ref[0].astype(BF16))

        @pl.when(c == 0)
        def _():
            acc_scr[...] = part

        @pl.when((c > 0) & (c < last_c))
        def _():
            acc_scr[...] += part

        @pl.when(c == last_c)
        def _():
            @pl.when(i > 0)
            def _():
                tile_copy(y_buf, y_hbm, sem_out).wait()
            _store_token_rows(y_buf, acc_scr[...] + part)

            def body(r, carry):
                _token_copy(y_buf, r * TOKEN_ROWS, y_hbm, dst_ref[r], sem_out).start(ROW_DMA_PRIORITY)
                return carry
            lax.fori_loop(0, TM_E, body, 0, unroll=ROW_DMA_UNROLL)

            @pl.when(i == n_tiles - 1)
            def _():
                tile_copy(y_buf, y_hbm, sem_out).wait()


def _experts(tile_expert, n_tiles, slot_src, slot_dst, h_all, wgu, wd, n_out_tokens):
    rows = slot_src.shape[0]
    t = rows // TM_E
    n_c = D_FF_E // FF_CHUNK_E
    assert n_c > 1
    live = lambda i, nt: jnp.maximum(jnp.minimum(i, nt[0] - 1), 0)
    smem_rows = lambda f: pl.BlockSpec((TM_E,), f, memory_space=pltpu.SMEM)
    grid_spec = pltpu.PrefetchScalarGridSpec(
        num_scalar_prefetch=2,
        grid=(t, n_c),
        in_specs=[smem_rows(lambda i, c, te, nt: (0,)),
                  smem_rows(lambda i, c, te, nt: (jnp.minimum(i + 1, t - 1),)),
                  smem_rows(lambda i, c, te, nt: (i,)),
                  pl.BlockSpec(memory_space=pl.ANY),
                  pl.BlockSpec((1, D_MODEL, FF_CHUNK_E), lambda i, c, te, nt: (te[live(i, nt)], 0, c)),
                  pl.BlockSpec((1, D_MODEL, FF_CHUNK_E), lambda i, c, te, nt: (te[live(i, nt)], 0, n_c + c)),
                  pl.BlockSpec((1, FF_CHUNK_E, D_MODEL), lambda i, c, te, nt: (te[live(i, nt)], c, 0))],
        out_specs=pl.BlockSpec(memory_space=pl.ANY),
        scratch_shapes=[pltpu.VMEM((2, TM_E * TOKEN_ROWS, LANES), F32), pltpu.VMEM((TM_E * TOKEN_ROWS, LANES), F32),
                        pltpu.VMEM((TM_E, D_MODEL), BF16), pltpu.VMEM((TM_E, D_MODEL), F32),
                        pltpu.SemaphoreType.DMA((2,)), pltpu.SemaphoreType.DMA(())],
    )
    return pl.pallas_call(
        _experts_kernel,
        out_shape=jax.ShapeDtypeStruct(((n_out_tokens + TM_E) * TOKEN_ROWS, LANES), F32),
        grid_spec=grid_spec,
        compiler_params=pltpu.CompilerParams(dimension_semantics=("arbitrary", "arbitrary"),
                                             vmem_limit_bytes=VMEM_LIMIT, has_side_effects=True),
        name="experts",
    )(tile_expert, n_tiles, slot_src, slot_src, slot_dst, h_all, wgu, wgu, wd)


def _combine_kernel(x_ref, ya_ref, yb_ref, route_ref, mod_ref, g_ref, o_ref):
    w0 = route_ref[:, ROUTE_W0:ROUTE_W0 + 1]
    w1 = route_ref[:, ROUTE_W1:ROUTE_W1 + 1]
    rows = x_ref.shape[0]
    x = x_ref[...] + mod_ref[0, 5:6, :] * (w0 * _load_token_rows(ya_ref, rows) + w1 * _load_token_rows(yb_ref, rows))
    o_ref[...] = x * lax.rsqrt(jnp.mean(x * x, axis=-1, keepdims=True) + EPS) * g_ref[...]


def _combine(x3, y2, route, first_row, n_rows, n_tok, mod_l, cond_of_tile, final_g):
    nt = n_rows // TM
    blk0 = first_row // TM
    blk1 = (n_tok + first_row) // TM
    return pl.pallas_call(
        _combine_kernel,
        out_shape=jax.ShapeDtypeStruct((n_rows, D_MODEL), F32),
        grid=(nt,),
        in_specs=[pl.BlockSpec((TM, D_MODEL), lambda i: (blk0 + i, 0)),
                  pl.BlockSpec((TM * TOKEN_ROWS, LANES), lambda i: (blk0 + i, 0)),
                  pl.BlockSpec((TM * TOKEN_ROWS, LANES), lambda i: (blk1 + i, 0)),
                  pl.BlockSpec((TM, LANES), lambda i: (blk0 + i, 0)),
                  pl.BlockSpec((1, 6, D_MODEL), lambda i: (cond_of_tile(blk0 + i), 0, 0)),
                  _const_spec((1, D_MODEL))],
        out_specs=pl.BlockSpec((TM, D_MODEL), lambda i: (i, 0)),
        compiler_params=_cparams(1),
        name="combine",
    )(x3, y2, y2, route, mod_l, final_g)


def _even_params(ev_w_in, ev_conv_a_w, ev_conv_a_b, ev_a_log, ev_dt_bias, ev_d_skip, ev_gnorm,
                 ev_conv_b_w, ev_conv_b_b, ev_lru_w, ev_lru_b, ev_lru_lam, ev_w_out, ev_ffn_gu, ev_ffn_down):
    d_a = H_A * P_A
    xbc = d_a + 2 * G_A * N_A
    o_xbc, o_dt, o_gate, o_xb = d_a, d_a + xbc, d_a + xbc + 2 * H_A, d_a + xbc + 2 * H_A + D_MODEL
    w = ev_w_in
    w_main = jnp.concatenate([w[:, :d_a], w[:, o_xb:], w[:, o_gate:o_xb], w[:, o_xbc:o_dt]], axis=1).astype(BF16)
    w_dt = jnp.pad(w[:, o_dt:o_gate], ((0, 0), (0, LANES - 2 * H_A))).astype(BF16)
    zeros_w = jnp.zeros((CONV_K, D_MODEL), F32)
    cw = jnp.concatenate([zeros_w, ev_conv_b_w, zeros_w, ev_conv_a_w], axis=1)
    zeros_b = jnp.zeros((D_MODEL,), F32)
    cb = jnp.concatenate([zeros_b, ev_conv_b_b, zeros_b, ev_conv_a_b])[None, :]
    dtb = jnp.pad(ev_dt_bias.reshape(-1), (0, LANES - 2 * H_A))[None, :]
    alog = jnp.pad(ev_a_log.reshape(-1), (0, LANES - 2 * H_A))[None, :]
    j = jnp.arange(2 * LANES) % LANES
    c = jnp.arange(d_a) // P_A
    e2 = jnp.stack([(j[:, None] == (d * H_A + c)[None, :]) for d in range(2)]).astype(BF16)
    wl = jnp.transpose(ev_lru_w, (2, 3, 0, 1, 4)).reshape(H_B, BW_B, 4 * BW_B).astype(BF16)
    lb = jnp.transpose(ev_lru_b.reshape(2, 2, H_B, BW_B), (2, 0, 1, 3)).reshape(H_B, 1, 4 * BW_B)
    lam = jnp.transpose(ev_lru_lam.reshape(2, H_B, BW_B), (1, 0, 2))
    dsk = jnp.repeat(ev_d_skip, P_A)[None, :]
    return dict(w_main=w_main, w_dt=w_dt, cw=cw, cb=cb, dtb=dtb, alog=alog, e2=e2, wl=wl, lb=lb, lam=lam,
                dsk=dsk, gn=ev_gnorm[None, :], w_out=ev_w_out.astype(BF16),
                wgu=ev_ffn_gu.astype(BF16), wd=ev_ffn_down.astype(BF16))


def _rope_partner(w, lo):
    blk = w[:, lo:lo + ROPE_C].reshape(w.shape[0], 2, 2, N_FREQ)
    return jnp.flip(blk, axis=2).reshape(w.shape[0], ROPE_C)


def _odd_params(od_w_in, od_q_norm, od_w_q_up, od_kv_norm, od_w_kv_up, od_w_o, od_router):
    kr_lo = Q_RANK + KV_RANK
    wi = jnp.zeros((D_MODEL, OD_IN_COLS), F32)
    wi = wi.at[:, :kr_lo + ROPE_C].set(od_w_in)
    wi = wi.at[:, 768:768 + ROPE_C].set(_rope_partner(od_w_in, kr_lo))
    wq3 = od_w_q_up.reshape(Q_RANK, H_C, NOPE_C + ROPE_C)
    pad = HEAD_PAD - NOPE_C - ROPE_C
    wq = jnp.pad(wq3, ((0, 0), (0, 0), (0, pad))).reshape(Q_RANK, H_C * HEAD_PAD)
    partner = jnp.flip(wq3[:, :, NOPE_C:].reshape(Q_RANK, H_C, 2, 2, N_FREQ), axis=3).reshape(Q_RANK, H_C, ROPE_C)
    wqs = jnp.pad(partner, ((0, 0), (0, 0), (NOPE_C, pad))).reshape(Q_RANK, H_C * HEAD_PAD)
    wkv3 = od_w_kv_up.reshape(KV_RANK, H_C, NOPE_C + V_C)
    wk = jnp.pad(wkv3[:, :, :NOPE_C], ((0, 0), (0, 0), (0, HEAD_PAD - NOPE_C))).reshape(KV_RANK, H_C * HEAD_PAD)
    wv_t = wkv3[:, :, NOPE_C:].reshape(KV_RANK, H_C * V_C).T
    ek_head = jnp.pad(jnp.eye(ROPE_C, dtype=F32), ((0, 0), (NOPE_C, pad)))
    ek = jnp.tile(ek_head, (1, H_C))
    wr_f = jnp.pad(od_router, ((0, 0), (0, LANES - N_EXPERTS)))
    wr_hi = wr_f.astype(BF16)
    wr = jnp.stack([wr_hi, (wr_f - wr_hi.astype(F32)).astype(BF16)])
    return dict(wi=wi.astype(BF16), qn=od_q_norm[None, :], kvn=od_kv_norm[None, :], wq=wq.astype(BF16),
                wqs=wqs.astype(BF16), wk=wk.astype(BF16), ek=ek.astype(BF16), wv_t=wv_t.astype(BF16),
                wo=od_w_o.astype(BF16), wr=wr)


def _rope_tables(n_tokens):
    rows = n_tokens // GRID_W
    row = jnp.repeat(jnp.arange(rows), GRID_W).astype(F32)
    col = jnp.tile(jnp.arange(GRID_W), rows).astype(F32)
    inv = ROPE_BASE ** (-jnp.arange(N_FREQ, dtype=F32) / N_FREQ)
    ang_r, ang_c = row[:, None] * inv, col[:, None] * inv
    cos_k = jnp.concatenate([jnp.cos(ang_r)] * 2 + [jnp.cos(ang_c)] * 2, axis=1)
    sin_k = jnp.concatenate([-jnp.sin(ang_r), jnp.sin(ang_r), -jnp.sin(ang_c), jnp.sin(ang_c)], axis=1)
    pad = HEAD_PAD - NOPE_C - ROPE_C
    cos_q = jnp.pad(cos_k, ((0, 0), (NOPE_C, pad)), constant_values=1.0)
    sin_q = jnp.pad(sin_k, ((0, 0), (NOPE_C, pad)))
    return cos_q, sin_q, cos_k, sin_k


def _layer0(x2d, mod_l, cond_row, norm_g0, ev, n_batch, seq_len, h0_ssd, h0_lru, is_ctx):
    per_block = max(ROW_BLOCK // seq_len, 1)
    blocks_per_seq = max(seq_len // ROW_BLOCK, 1)
    cond_of_block = (lambda i: 0) if is_ctx else (lambda i: cond_row(i // blocks_per_seq))
    tiles_per_seq = seq_len // TM if seq_len >= TM else 1
    cond_of_tile = (lambda i: 0) if is_ctx else (lambda i: cond_row(i // tiles_per_seq))
    p, dt = _ev_inproj(x2d, mod_l, cond_of_block, norm_g0[0:1], ev["w_main"], ev["w_dt"], ev["cw"], ev["cb"],
                       ev["dtb"], seq_len)
    ssd_out = _ssd(p, dt, ev["alog"], ev["e2"], h0_ssd, n_batch, seq_len, emit_final=is_ctx)
    lru_out = _lru(p, ev["wl"], ev["lb"], ev["lam"], h0_lru, chain=not is_ctx, emit_final=is_ctx)
    x1 = _ev_out(x2d, ssd_out[0], ssd_out[1], p, lru_out[0], ev["w_out"], ev["dsk"], ev["gn"], mod_l, cond_of_tile)
    x2 = _ffn(x1, mod_l, cond_of_tile, norm_g0[1:2], ev["wgu"], ev["wd"])
    finals = None
    if is_ctx:
        hp = H_A * P_A
        s_ssd = ssd_out[2].reshape(n_batch, 1, 2, H_A, P_A, N_A)
        fin = lru_out[1]
        s_lru = jnp.transpose(fin, (0, 3, 2, 1, 4)).reshape(n_batch, 1, 2, H_B * BW_B)
        finals = (s_ssd, s_lru)
        del hp, per_block
    return x2, finals, cond_of_tile


def _layer1_pre(x2d, mod_l, cond_of_tile, norm_g1, od, n_batch, seq_len, rope, cache):
    q, ckv, kr = _od_inproj(x2d, mod_l, cond_of_tile, norm_g1[0:1], od["wi"], od["qn"], od["kvn"], od["wq"],
                            rope, seq_len)
    if cache is None:
        ckv_all, kr_all, kv_len = ckv, kr, seq_len
    else:
        c_ckv, c_kr = cache
        past = c_ckv.shape[1]
        kv_len = past + seq_len
        ckv_all = jnp.concatenate([c_ckv, ckv.reshape(n_batch, seq_len, KV_RANK)], axis=1).reshape(-1, KV_RANK)
        kr_all = jnp.concatenate([c_kr, kr.reshape(n_batch, seq_len, ROPE_C)], axis=1).reshape(-1, ROPE_C)
    k, v_t = _kv_up(ckv_all, kr_all, od["wk"], od["ek"], od["wv_t"])
    o = _attention(q, k, v_t, n_batch, seq_len, kv_len)
    return o, ckv, kr


def kernel(x_prompt, x_sample, state_ssd, state_lru, cache_ckv, cache_krope, c, c_ctx, mod_w, mod_b, norm_g, final_g, ev_w_in, ev_conv_a_w, ev_conv_a_b, ev_a_log, ev_dt_bias, ev_d_skip, ev_gnorm, ev_conv_b_w, ev_conv_b_b, ev_lru_w, ev_lru_b, ev_lru_lam, ev_w_out, ev_ffn_gu, ev_ffn_down, od_w_in, od_q_norm, od_w_q_up, od_kv_norm, od_w_kv_up, od_w_o, od_router, od_moe_gu, od_moe_down):
    batch, seq, _ = x_prompt.shape
    dbatch, dseq, _ = x_sample.shape
    n_ctx, n_dec = batch * seq, dbatch * dseq

    cond = jnp.concatenate([c_ctx[None, :], c, jnp.zeros((16 - 1 - dbatch, D_MODEL), F32)], axis=0)
    mod = _modulation(cond, mod_w, mod_b)
    ev = _even_params(ev_w_in[0], ev_conv_a_w[0], ev_conv_a_b[0], ev_a_log[0], ev_dt_bias[0], ev_d_skip[0],
                      ev_gnorm[0], ev_conv_b_w[0], ev_conv_b_b[0], ev_lru_w[0], ev_lru_b[0], ev_lru_lam[0],
                      ev_w_out[0], ev_ffn_gu[0], ev_ffn_down[0])
    od = _odd_params(od_w_in[0], od_q_norm[0], od_w_q_up[0], od_kv_norm[0], od_w_kv_up[0], od_w_o[0], od_router[0])
    wgu_e = od_moe_gu[0]
    wd_e = od_moe_down[0]
    dec_row = lambda b: 1 + b

    xc, finals, tile_c = _layer0(x_prompt.reshape(n_ctx, D_MODEL), mod[0], None, norm_g[0], ev, batch, seq,
                                 None, None, True)
    h0_ssd = state_ssd[:, 0].reshape(dbatch, 2, H_A * P_A, N_A)
    h0_lru = jnp.transpose(state_lru[:, 0].reshape(dbatch, 2, H_B, BW_B), (0, 2, 1, 3))
    xd, _, tile_d = _layer0(x_sample.reshape(n_dec, D_MODEL), mod[0], dec_row, norm_g[0], ev, dbatch, dseq,
                            h0_ssd, h0_lru, False)
    new_state_ssd, new_state_lru = finals

    rope = (od["wqs"],) + _rope_tables(dseq)
    o_c, ckv_c, kr_c = _layer1_pre(xc, mod[1], tile_c, norm_g[1], od, batch, seq, None, None)
    o_d, _, _ = _layer1_pre(xd, mod[1], tile_d, norm_g[1], od, dbatch, dseq, rope,
                            (cache_ckv[:, 0], cache_krope[:, 0]))
    new_cache_ckv = ckv_c.reshape(batch, 1, seq, KV_RANK)
    new_cache_krope = kr_c.reshape(batch, 1, seq, ROPE_C)

    tiles_c = n_ctx // TM
    tile_all = lambda i: jnp.where(i < tiles_c, 0, tile_d(jnp.maximum(i - tiles_c, 0)))
    x3, h2, route, counts = _od_out_router(o_c, o_d, xc, xd, od["wo"], mod[1], tile_all, norm_g[1][1:2], od["wr"])
    y2 = _routed_experts(h2, route, counts, wgu_e, wd_e)
    n_tok = n_ctx + n_dec
    fg = final_g[None, :]
    y_prompt = _combine(x3, y2, route, 0, n_ctx, n_tok, mod[1], tile_all, fg).reshape(batch, seq, D_MODEL)
    y_sample = _combine(x3, y2, route, n_ctx, n_dec, n_tok, mod[1], tile_all, fg).reshape(dbatch, dseq, D_MODEL)
    return (y_prompt, y_sample, new_state_ssd, new_state_lru, new_cache_ckv, new_cache_krope)


def _routed_experts(h2, route, counts, wgu_e, wd_e):
    n_tok = route.shape[0]
    n_slots = (2 * n_tok // TM_E + N_EXPERTS) * TM_E
    cnt_e = counts[0, :N_EXPERTS].astype(jnp.int32)
    tiles_e = (cnt_e + TM_E - 1) // TM_E
    tile_end = jnp.cumsum(tiles_e)
    slot_start = (tile_end - tiles_e) * TM_E
    n_tiles = tile_end[-1:].astype(jnp.int32)
    tile_id = jnp.arange(n_slots // TM_E, dtype=jnp.int32)
    tile_expert = jnp.minimum(jnp.sum((tile_end[None, :] <= tile_id[:, None]).astype(jnp.int32), axis=1),
                              N_EXPERTS - 1)
    idx = route[:, ROUTE_IDX0:ROUTE_IDX1 + 1].astype(jnp.int32)
    rank = route[:, ROUTE_RANK0:ROUTE_RANK1 + 1].astype(jnp.int32)
    slot = slot_start[idx] + rank
    out_tok = jnp.arange(n_tok, dtype=jnp.int32)[:, None] + jnp.array([0, n_tok], jnp.int32)[None, :]
    spare = 2 * n_tok + jnp.arange(n_slots, dtype=jnp.int32) % TM_E
    slot_dst = spare.at[slot.reshape(-1)].set(out_tok.reshape(-1))
    slot_src = jnp.where(slot_dst >= 2 * n_tok, 0, jnp.where(slot_dst >= n_tok, slot_dst - n_tok, slot_dst))
    return _experts(tile_expert, n_tiles, slot_src * TOKEN_ROWS, slot_dst * TOKEN_ROWS, h2, wgu_e, wd_e, 2 * n_tok)
```

```python
import functools
import math

import jax
import jax.numpy as jnp
from jax import lax
from jax.experimental import pallas as pl
from jax.experimental.pallas import tpu as pltpu

F32 = jnp.float32
BF16 = jnp.bfloat16
HIGHEST = lax.Precision.HIGHEST

D_MODEL = 1024
GRID_W = 64
P_A = 64
H_A = 16
G_A = 2
N_A = 128
CHUNK = 128
CONV_K = 4
H_B = 8
BW_B = 128
LRU_C = 8.0
H_C = 16
Q_RANK = 384
KV_RANK = 256
NOPE_C = 64
ROPE_C = 32
V_C = 64
N_FREQ = ROPE_C // 4
ROPE_BASE = 10000.0
D_FF = 2816
N_EXPERTS = 8
D_FF_E = 3584
EPS = 1e-6

LANES = 128
HEAD_PAD = 128
ROW_BLOCK = 2048
ROW_CHUNK = 256
SEG_PITCH = ROW_CHUNK + 8
TM = 512
TQ_MAX = 256
SSD_CHUNKS_PER_STEP = 4
LOG2_E = 1.4426950408889634
TM_E = 1024
FF_CHUNK_E = 512
VMEM_LIMIT = 52 * 1024 * 1024


def _cparams(n_axes, vmem=VMEM_LIMIT):
    return pltpu.CompilerParams(dimension_semantics=("arbitrary",) * n_axes, vmem_limit_bytes=vmem)


def _dot(a, b):
    return jnp.dot(a, b, preferred_element_type=F32)


def _sigmoid(x):
    return 1.0 / (1.0 + jnp.exp(-x))


def _silu(x):
    return x * _sigmoid(x)


def _softplus(x):
    return jnp.maximum(x, 0.0) + jnp.log(1.0 + jnp.exp(-jnp.abs(x)))


def _gelu_tanh(x):
    return 0.5 * x * (1.0 + jnp.tanh(math.sqrt(2.0 / math.pi) * (x + 0.044715 * (x * x * x))))


def _modulated_norm(x, gain, scale, shift):
    ms = jnp.mean(x * x, axis=-1, keepdims=True)
    return (x * lax.rsqrt(ms + EPS)) * (gain * (1.0 + scale)) + shift


def _const_spec(shape):
    zeros = (0,) * len(shape)
    return pl.BlockSpec(shape, lambda *_: zeros, pipeline_mode=pl.Buffered(1))


def _mod_kernel(c_ref, w_ref, b_ref, o_ref):
    c = c_ref[...]
    o_ref[0] = _dot(_silu(c).astype(BF16), w_ref[0].astype(BF16)) + b_ref[0]


def _modulation(cond, mod_w, mod_b):
    depth = mod_w.shape[0]
    n = 6 * D_MODEL
    tn = n // 4
    out = pl.pallas_call(
        _mod_kernel,
        out_shape=jax.ShapeDtypeStruct((depth, 16, n), F32),
        grid=(depth, 4),
        in_specs=[pl.BlockSpec((16, D_MODEL), lambda l, j: (0, 0)),
                  pl.BlockSpec((1, D_MODEL, tn), lambda l, j: (l, 0, j)),
                  pl.BlockSpec((1, 1, tn), lambda l, j: (l, 0, j))],
        out_specs=pl.BlockSpec((1, 16, tn), lambda l, j: (l, 0, j)),
        compiler_params=_cparams(2),
        name="modulation",
    )(cond, mod_w, mod_b.reshape(depth, 1, n))
    return out.reshape(depth, 16, 6, D_MODEL)


def _ev_inproj_kernel(x_ref, mod_ref, g_ref, w_ref, wdt_ref, cw_ref, cb_ref, dtb_ref,
                      p_ref, dt_ref, h_scr, *, seq_is_chunk):
    j = pl.program_id(1)
    rows = h_scr.shape[0]
    n_chunks = rows // ROW_CHUNK
    width = w_ref.shape[1]

    @pl.when(j == 0)
    def _():
        gain, scale, shift = g_ref[...], mod_ref[0, 1:2, :], mod_ref[0, 0:1, :]

        def body(r, carry):
            r0 = pl.multiple_of(r * ROW_CHUNK, ROW_CHUNK)
            h = _modulated_norm(x_ref[pl.ds(r0, ROW_CHUNK), :], gain, scale, shift)
            h_scr[pl.ds(r0, ROW_CHUNK), :] = h.astype(BF16)
            return carry

        lax.fori_loop(0, n_chunks, body, 0)
        dt = _softplus(_dot(h_scr[...], wdt_ref[...]) + dtb_ref[...])
        lane = lax.broadcasted_iota(jnp.int32, dt.shape, 1)
        dt_ref[...] = jnp.where(lane < 2 * H_A, dt, 0.0)

    def plain(act):
        acc = _dot(h_scr[...], w_ref[...])
        for r in range(n_chunks):
            r0 = r * ROW_CHUNK
            p_ref[r0:r0 + ROW_CHUNK, :] = act(acc[r0:r0 + ROW_CHUNK]).astype(BF16)

    def conv(act):
        acc = _dot(h_scr[...], w_ref[...])
        win = ROW_CHUNK + 16
        cw = cw_ref[...]
        cb = cb_ref[...]
        halo = jnp.zeros((8, width), F32)
        for r in range(n_chunks):
            r0 = r * ROW_CHUNK
            first = seq_is_chunk or r == 0
            last = seq_is_chunk or r == n_chunks - 1
            xw = jnp.concatenate([halo if first else acc[r0 - 8:r0], acc[r0:r0 + ROW_CHUNK],
                                  halo if last else acc[r0 + ROW_CHUNK:r0 + ROW_CHUNK + 8]], axis=0)
            y = cb + cw[1:2, :] * acc[r0:r0 + ROW_CHUNK]
            for k in (0, 2, 3):
                sh = pltpu.roll(xw, (1 - k) % win, 0)[8:ROW_CHUNK + 8]
                y = y + cw[k:k + 1, :] * sh
            p_ref[r0:r0 + ROW_CHUNK, :] = act(y).astype(BF16)

    @pl.when(j < 2)
    def _():
        plain(lambda v: v)

    @pl.when((j >= 2) & (j < 4))
    def _():
        conv(lambda v: v)

    @pl.when((j >= 4) & (j < 6))
    def _():
        plain(_gelu_tanh)

    @pl.when(j >= 6)
    def _():
        conv(_silu)


P_COLS = 4608


def _ev_inproj(x2d, mod_l, cond_of_block, gain, w, wdt, cw, cb, dtb, seq_len):
    n = x2d.shape[0]
    nblk = n // ROW_BLOCK
    tn = 512
    kern = functools.partial(_ev_inproj_kernel, seq_is_chunk=(seq_len == ROW_CHUNK))
    return pl.pallas_call(
        kern,
        out_shape=(jax.ShapeDtypeStruct((n, P_COLS), BF16), jax.ShapeDtypeStruct((n, LANES), F32)),
        grid=(nblk, P_COLS // tn),
        in_specs=[pl.BlockSpec((ROW_BLOCK, D_MODEL), lambda i, j: (i, 0)),
                  pl.BlockSpec((1, 6, D_MODEL), lambda i, j: (cond_of_block(i), 0, 0)),
                  pl.BlockSpec((1, D_MODEL), lambda i, j: (0, 0)),
                  pl.BlockSpec((D_MODEL, tn), lambda i, j: (0, j)),
                  pl.BlockSpec((D_MODEL, LANES), lambda i, j: (0, 0)),
                  pl.BlockSpec((CONV_K, tn), lambda i, j: (0, j)),
                  pl.BlockSpec((1, tn), lambda i, j: (0, j)),
                  pl.BlockSpec((1, LANES), lambda i, j: (0, 0))],
        out_specs=(pl.BlockSpec((ROW_BLOCK, tn), lambda i, j: (i, j)),
                   pl.BlockSpec((ROW_BLOCK, LANES), lambda i, j: (i, 0))),
        scratch_shapes=[pltpu.VMEM((ROW_BLOCK, D_MODEL), BF16)],
        compiler_params=_cparams(2),
        name="ev_inproj",
    )(x2d, mod_l, gain, w, wdt, cw, cb, dtb)


def _ssd_direction(d, xs_ref, bm_ref, cm_ref, dt_ref, arate, e2, h_scr, y_ref):
    xs = xs_ref[...].astype(F32)
    bm = bm_ref[...]
    cm = cm_ref[...]
    a = dt_ref[...] * arate
    li = lax.broadcasted_iota(jnp.int32, (CHUNK, CHUNK), 0)
    si = lax.broadcasted_iota(jnp.int32, (CHUNK, CHUNK), 1)
    keep = (si <= li) if d == 0 else (si >= li)
    tri = jnp.where(keep, 1.0, 0.0).astype(F32)
    acs = jnp.dot(tri, a, precision=HIGHEST, preferred_element_type=F32)
    acs_t = acs.T
    tot = acs[CHUNK - 1:CHUNK, :] if d == 0 else acs[0:1, :]
    stack = jnp.concatenate([dt_ref[...], jnp.exp(acs), jnp.exp(tot - acs)], axis=0)
    hi = stack.astype(BF16)
    lo = (stack - hi.astype(F32)).astype(BF16)
    ex = _dot(jnp.concatenate([hi, lo], axis=1), e2)
    dt_e, eacs_e, ds_e = ex[0:CHUNK], ex[CHUNK:2 * CHUNK], ex[2 * CHUNK:3 * CHUNK]
    xdt = xs * dt_e
    xdt_b = xdt.astype(BF16)
    xds_b = (xdt * ds_e).astype(BF16)
    lane = lax.broadcasted_iota(jnp.int32, (CHUNK, LANES), 1)
    gw = (H_A // G_A) * P_A
    y_groups = []
    for g in range(G_A):
        bm_g = bm[:, g * N_A:(g + 1) * N_A]
        cm_g = cm[:, g * N_A:(g + 1) * N_A]
        cb = lax.dot_general(cm_g, bm_g, (((1,), (1,)), ((), ())), preferred_element_type=F32)
        h_prev = h_scr[d, :, g * gw:(g + 1) * gw]
        y_off = _dot(cm_g, h_prev.astype(BF16)) * eacs_e[:, g * gw:(g + 1) * gw]
        bm_t = bm_g.astype(F32).T.astype(BF16)
        st = _dot(bm_t, xds_b[:, g * gw:(g + 1) * gw])
        decay_tot = eacs_e[CHUNK - 1:CHUNK, g * gw:(g + 1) * gw] if d == 0 else eacs_e[0:1, g * gw:(g + 1) * gw]
        h_scr[d, :, g * gw:(g + 1) * gw] = decay_tot * h_prev + st
        pairs = []
        for q in range(H_A // G_A // 2):
            ms = []
            for e in (g * 8 + 2 * q, g * 8 + 2 * q + 1):
                col = acs[:, d * H_A + e:d * H_A + e + 1]
                row = acs_t[d * H_A + e:d * H_A + e + 1, :]
                dec = jnp.where(keep, jnp.exp(jnp.minimum(col - row, 0.0)), 0.0)
                ms.append((cb * dec).astype(BF16))
            lhs = jnp.concatenate(ms, axis=1)
            xp = xdt_b[:, (g * 8 + 2 * q) * P_A:(g * 8 + 2 * q + 2) * P_A]
            zero = jnp.zeros_like(xp)
            rhs = jnp.concatenate([jnp.where(lane < P_A, xp, zero), jnp.where(lane >= P_A, xp, zero)], axis=0)
            pairs.append(_dot(lhs, rhs))
        y_groups.append(jnp.concatenate(pairs, axis=1) + y_off)
    y_ref[...] = jnp.concatenate(y_groups, axis=1).astype(y_ref.dtype)


def _ssd_kernel(*refs, has_h0, emit_final):
    (xsf, bmf, cmf, dtf, xsb, bmb, cmb, dtb, alog_ref, e2_ref), rest = refs[:10], refs[10:]
    if has_h0:
        h0_ref, rest = rest[0], rest[1:]
    yf_ref, yb_ref, rest = rest[0], rest[1], rest[2:]
    if emit_final:
        hfin_ref, rest = rest[0], rest[1:]
    h_scr = rest[0]
    i = pl.program_id(1)

    @pl.when(i == 0)
    def _():
        for d in range(2):
            if has_h0:
                h_scr[d] = h0_ref[0, d].T
            else:
                h_scr[d] = jnp.zeros(h_scr.shape[1:], F32)

    arate = -jnp.exp(alog_ref[...])
    chunks_per_step = xsf.shape[0] // CHUNK
    for s in range(chunks_per_step):
        sf = pl.ds(s * CHUNK, CHUNK)
        sb = pl.ds((chunks_per_step - 1 - s) * CHUNK, CHUNK)
        _ssd_direction(0, xsf.at[sf], bmf.at[sf], cmf.at[sf], dtf.at[sf], arate, e2_ref[0], h_scr, yf_ref.at[sf])
        _ssd_direction(1, xsb.at[sb], bmb.at[sb], cmb.at[sb], dtb.at[sb], arate, e2_ref[1], h_scr, yb_ref.at[sb])

    if emit_final:
        @pl.when(i == pl.num_programs(1) - 1)
        def _():
            for d in range(2):
                hfin_ref[0, d] = h_scr[d].T


def _ssd(p, dt, alog_row, e2, h0, n_batch, seq_len, emit_final):
    n = p.shape[0]
    rows = min(SSD_CHUNKS_PER_STEP * CHUNK, seq_len)
    nc = seq_len // rows
    hp = H_A * P_A
    fwd = lambda b, i: b * nc + i
    bwd = lambda b, i: b * nc + (nc - 1 - i)
    xs_blk, bm_blk, cm_blk = 3, 16, 17

    def specs(rowf):
        return [pl.BlockSpec((rows, hp), lambda b, i: (rowf(b, i), xs_blk)),
                pl.BlockSpec((rows, G_A * N_A), lambda b, i: (rowf(b, i), bm_blk)),
                pl.BlockSpec((rows, G_A * N_A), lambda b, i: (rowf(b, i), cm_blk)),
                pl.BlockSpec((rows, LANES), lambda b, i: (rowf(b, i), 0))]

    in_specs = specs(fwd) + specs(bwd) + [pl.BlockSpec((1, LANES), lambda b, i: (0, 0)),
                                          pl.BlockSpec((2, 2 * LANES, hp), lambda b, i: (0, 0, 0))]
    args = [p, p, p, dt, p, p, p, dt, alog_row, e2]
    if h0 is not None:
        in_specs.append(pl.BlockSpec((1, 2, hp, N_A), lambda b, i: (b, 0, 0, 0)))
        args.append(h0)
    out_shape = [jax.ShapeDtypeStruct((n, hp), BF16), jax.ShapeDtypeStruct((n, hp), BF16)]
    out_specs = [pl.BlockSpec((rows, hp), lambda b, i: (fwd(b, i), 0)),
                 pl.BlockSpec((rows, hp), lambda b, i: (bwd(b, i), 0))]
    if emit_final:
        out_shape.append(jax.ShapeDtypeStruct((n_batch, 2, hp, N_A), F32))
        out_specs.append(pl.BlockSpec((1, 2, hp, N_A), lambda b, i: (b, 0, 0, 0)))
    kern = functools.partial(_ssd_kernel, has_h0=h0 is not None, emit_final=emit_final)
    return pl.pallas_call(
        kern, out_shape=tuple(out_shape), grid=(n_batch, nc),
        in_specs=in_specs, out_specs=tuple(out_specs),
        scratch_shapes=[pltpu.VMEM((2, N_A, hp), F32)],
        compiler_params=_cparams(2),
        name="ssd",
    )(*args)


def _lru_kernel(*refs, chain, emit_final):
    (xr_ref, gg_ref, wl_ref, lb_ref, lam_ref), rest = refs[:5], refs[5:]
    if chain:
        h0_ref, rest = rest[0], rest[1:]
    y_ref, rest = rest[0], rest[1:]
    if emit_final:
        fin_ref, rest = rest[0], rest[1:]
    a_scr, u_scr, h_scr, p_scr = rest
    rows = xr_ref.shape[0]
    n_seg = rows // ROW_CHUNK
    sp = _softplus(-lam_ref[0])

    def gates(r, carry):
        r0 = pl.multiple_of(r * ROW_CHUNK, ROW_CHUNK)
        s0 = pl.multiple_of(r * SEG_PITCH, 8)
        xr_b = xr_ref[pl.ds(r0, ROW_CHUNK), :]
        xr = xr_b.astype(F32)
        g = _dot(xr_b, wl_ref[0]) + lb_ref[0]
        for d in range(2):
            r_gate = _sigmoid(g[:, (2 * d) * BW_B:(2 * d + 1) * BW_B])
            i_gate = _sigmoid(g[:, (2 * d + 1) * BW_B:(2 * d + 2) * BW_B])
            a = jnp.exp((-LRU_C) * r_gate * sp[d:d + 1, :])
            u = jnp.sqrt(jnp.maximum(1.0 - a * a, 0.0)) * (i_gate * xr)
            a_scr[d, pl.ds(s0, ROW_CHUNK), :] = a
            u_scr[d, pl.ds(s0, ROW_CHUNK), :] = u
        return carry

    lax.fori_loop(0, n_seg, gates, 0, unroll=4)

    def step(t, carry):
        hf, hb, pf, pb = carry
        tb = ROW_CHUNK - 1 - t
        sf = pl.ds(t, n_seg, stride=SEG_PITCH)
        sb = pl.ds(tb, n_seg, stride=SEG_PITCH)
        af = a_scr[0, sf, :]
        ab = a_scr[1, sb, :]
        hf = af * hf + u_scr[0, sf, :]
        hb = ab * hb + u_scr[1, sb, :]
        h_scr[0, sf, :] = hf
        h_scr[1, sb, :] = hb
        if chain:
            pf = af * pf
            pb = ab * pb
            p_scr[0, sf, :] = pf
            p_scr[1, sb, :] = pb
        return hf, hb, pf, pb

    z = jnp.zeros((n_seg, BW_B), F32)
    o = jnp.ones((n_seg, BW_B), F32)
    lax.fori_loop(0, ROW_CHUNK, step, (z, z, o, o), unroll=8)

    if emit_final:
        fin_ref[0, 0, 0] = h_scr[0, pl.ds(ROW_CHUNK - 1, n_seg, stride=SEG_PITCH), :]
        fin_ref[0, 0, 1] = h_scr[1, pl.ds(0, n_seg, stride=SEG_PITCH), :]

    if chain:
        carry = h0_ref[0, 0, 0:1, :]
        for s in range(n_seg):
            sl = slice(s * SEG_PITCH, s * SEG_PITCH + ROW_CHUNK)
            h = h_scr[0, sl, :] + p_scr[0, sl, :] * carry
            h_scr[0, sl, :] = h
            carry = h[ROW_CHUNK - 1:ROW_CHUNK, :]
        carry = h0_ref[0, 0, 1:2, :]
        for s in reversed(range(n_seg)):
            sl = slice(s * SEG_PITCH, s * SEG_PITCH + ROW_CHUNK)
            h = h_scr[1, sl, :] + p_scr[1, sl, :] * carry
            h_scr[1, sl, :] = h
            carry = h[0:1, :]

    for s in range(n_seg):
        sl = slice(s * SEG_PITCH, s * SEG_PITCH + ROW_CHUNK)
        gg = gg_ref[s * ROW_CHUNK:(s + 1) * ROW_CHUNK, :].astype(F32)
        y_ref[s * ROW_CHUNK:(s + 1) * ROW_CHUNK, :] = (gg * (h_scr[0, sl, :] + h_scr[1, sl, :])).astype(BF16)


def _lru(p, wl, lb, lam, h0, chain, emit_final):
    n = p.shape[0]
    nblk = n // ROW_BLOCK
    n_seg = ROW_BLOCK // ROW_CHUNK
    xb_blk, gate_blk = D_MODEL // BW_B, 2 * D_MODEL // BW_B
    in_specs = [pl.BlockSpec((ROW_BLOCK, BW_B), lambda i, h: (i, xb_blk + h)),
                pl.BlockSpec((ROW_BLOCK, BW_B), lambda i, h: (i, gate_blk + h)),
                pl.BlockSpec((1, BW_B, 4 * BW_B), lambda i, h: (h, 0, 0)),
                pl.BlockSpec((1, 1, 4 * BW_B), lambda i, h: (h, 0, 0)),
                pl.BlockSpec((1, 2, BW_B), lambda i, h: (h, 0, 0))]
    args = [p, p, wl, lb, lam]
    if chain:
        in_specs.append(pl.BlockSpec((1, 1, 2, BW_B), lambda i, h: (i, h, 0, 0)))
        args.append(h0)
    out_shape = [jax.ShapeDtypeStruct((n, D_MODEL), BF16)]
    out_specs = [pl.BlockSpec((ROW_BLOCK, BW_B), lambda i, h: (i, h))]
    if emit_final:
        out_shape.append(jax.ShapeDtypeStruct((nblk, H_B, 2, n_seg, BW_B), F32))
        out_specs.append(pl.BlockSpec((1, 1, 2, n_seg, BW_B), lambda i, h: (i, h, 0, 0, 0)))
    seg_rows = n_seg * SEG_PITCH
    kern = functools.partial(_lru_kernel, chain=chain, emit_final=emit_final)
    return pl.pallas_call(
        kern, out_shape=tuple(out_shape), grid=(nblk, H_B),
        in_specs=in_specs, out_specs=tuple(out_specs),
        scratch_shapes=[pltpu.VMEM((2, seg_rows, BW_B), F32)] * 4,
        compiler_params=_cparams(2),
        name="lru",
    )(*args)


def _ev_out_kernel(x_ref, yf_ref, yb_ref, xs_ref, z_ref, yl_ref, wo_ref, dsk_ref, gn_ref, mod_ref, o_ref):
    y = yf_ref[...].astype(F32) + yb_ref[...].astype(F32) + dsk_ref[...] * xs_ref[...].astype(F32)
    y = y * _silu(z_ref[...].astype(F32))
    gw = D_MODEL // G_A
    parts = []
    for g in range(G_A):
        yg = y[:, g * gw:(g + 1) * gw]
        ms = jnp.mean(yg * yg, axis=-1, keepdims=True)
        parts.append((yg * lax.rsqrt(ms + EPS) * gn_ref[:, g * gw:(g + 1) * gw]).astype(BF16))
    mix = _dot(parts[0], wo_ref[0:gw, :]) + _dot(parts[1], wo_ref[gw:2 * gw, :])
    mix = mix + _dot(yl_ref[...], wo_ref[D_MODEL:2 * D_MODEL, :])
    o_ref[...] = x_ref[...] + mod_ref[0, 2:3, :] * mix


def _ev_out(x2d, yf, yb, p, ylru, wo, dsk, gn, mod_l, cond_of_tile):
    n = x2d.shape[0]
    row = lambda i: (i, 0)
    return pl.pallas_call(
        _ev_out_kernel,
        out_shape=jax.ShapeDtypeStruct((n, D_MODEL), F32),
        grid=(n // TM,),
        in_specs=[pl.BlockSpec((TM, D_MODEL), row), pl.BlockSpec((TM, D_MODEL), row),
                  pl.BlockSpec((TM, D_MODEL), row),
                  pl.BlockSpec((TM, D_MODEL), lambda i: (i, 3)),
                  pl.BlockSpec((TM, D_MODEL), lambda i: (i, 0)),
                  pl.BlockSpec((TM, D_MODEL), row),
                  _const_spec((2 * D_MODEL, D_MODEL)), _const_spec((1, D_MODEL)), _const_spec((1, D_MODEL)),
                  pl.BlockSpec((1, 6, D_MODEL), lambda i: (cond_of_tile(i), 0, 0))],
        out_specs=pl.BlockSpec((TM, D_MODEL), row),
        compiler_params=_cparams(1),
        name="ev_out",
    )(x2d, yf, yb, p, p, ylru, wo, dsk, gn, mod_l)


def _ffn_kernel(x_ref, mod_ref, g_ref, wgu_ref, wd_ref, o_ref):
    x = x_ref[...]
    h = _modulated_norm(x, g_ref[...], mod_ref[0, 4:5, :], mod_ref[0, 3:4, :]).astype(BF16)
    n_chunk = 2
    fc = D_FF // n_chunk
    f = None
    for c in range(n_chunk):
        g = _dot(h, wgu_ref[:, c * fc:(c + 1) * fc])
        u = _dot(h, wgu_ref[:, D_FF + c * fc:D_FF + (c + 1) * fc])
        part = _dot((_silu(g) * u).astype(BF16), wd_ref[c * fc:(c + 1) * fc, :])
        f = part if f is None else f + part
    o_ref[...] = x + mod_ref[0, 5:6, :] * f


def _ffn(x2d, mod_l, cond_of_tile, gain, wgu, wd):
    n = x2d.shape[0]
    return pl.pallas_call(
        _ffn_kernel,
        out_shape=jax.ShapeDtypeStruct((n, D_MODEL), F32),
        grid=(n // TM,),
        in_specs=[pl.BlockSpec((TM, D_MODEL), lambda i: (i, 0)),
                  pl.BlockSpec((1, 6, D_MODEL), lambda i: (cond_of_tile(i), 0, 0)),
                  _const_spec((1, D_MODEL)), _const_spec((D_MODEL, 2 * D_FF)), _const_spec((D_FF, D_MODEL))],
        out_specs=pl.BlockSpec((TM, D_MODEL), lambda i: (i, 0)),
        compiler_params=_cparams(1),
        name="ffn",
    )(x2d, mod_l, gain, wgu, wd)


OD_IN_COLS = 896


def _od_inproj_kernel(*refs, use_rope):
    (x_ref, mod_ref, g_ref, wi_ref, qn_ref, kvn_ref, wq_ref), rest = refs[:7], refs[7:]
    if use_rope:
        (wqs_ref, cq_ref, sq_ref, ck_ref, sk_ref), rest = rest[:5], rest[5:]
    q_ref, ckv_ref, kr_ref = rest
    h = _modulated_norm(x_ref[...], g_ref[...], mod_ref[0, 1:2, :], mod_ref[0, 0:1, :]).astype(BF16)
    proj = _dot(h, wi_ref[...])
    cq = proj[:, 0:Q_RANK]
    cqn = (cq * lax.rsqrt(jnp.mean(cq * cq, axis=-1, keepdims=True) + EPS) * qn_ref[...]).astype(BF16)
    ckv = proj[:, Q_RANK:Q_RANK + KV_RANK]
    ckv_ref[...] = ckv * lax.rsqrt(jnp.mean(ckv * ckv, axis=-1, keepdims=True) + EPS) * kvn_ref[...]
    kr = proj[:, 640:640 + ROPE_C]
    q = _dot(cqn, wq_ref[...])
    if use_rope:
        q = q * jnp.tile(cq_ref[...], (1, H_C)) + _dot(cqn, wqs_ref[...]) * jnp.tile(sq_ref[...], (1, H_C))
        kr = kr * ck_ref[...] + proj[:, 768:768 + ROPE_C] * sk_ref[...]
    q = (q * ((NOPE_C + ROPE_C) ** -0.5 * LOG2_E)).astype(BF16)
    for h in range(H_C):
        q_ref[h] = q[:, h * HEAD_PAD:(h + 1) * HEAD_PAD]
    kr_ref[...] = kr


def _od_inproj(x2d, mod_l, cond_of_tile, gain, wi, qn, kvn, wq, rope, seq_len):
    n = x2d.shape[0]
    use_rope = rope is not None
    in_specs = [pl.BlockSpec((TM, D_MODEL), lambda i: (i, 0)),
                pl.BlockSpec((1, 6, D_MODEL), lambda i: (cond_of_tile(i), 0, 0)),
                _const_spec((1, D_MODEL)), _const_spec((D_MODEL, OD_IN_COLS)),
                _const_spec((1, Q_RANK)), _const_spec((1, KV_RANK)),
                _const_spec((Q_RANK, H_C * HEAD_PAD))]
    args = [x2d, mod_l, gain, wi, qn, kvn, wq]
    if use_rope:
        wqs, cos_q, sin_q, cos_k, sin_k = rope
        per_seq = seq_len // TM
        pos = lambda i: (i % per_seq, 0)
        in_specs += [_const_spec((Q_RANK, H_C * HEAD_PAD)),
                     pl.BlockSpec((TM, HEAD_PAD), pos), pl.BlockSpec((TM, HEAD_PAD), pos),
                     pl.BlockSpec((TM, ROPE_C), pos), pl.BlockSpec((TM, ROPE_C), pos)]
        args += [wqs, cos_q, sin_q, cos_k, sin_k]
    return pl.pallas_call(
        functools.partial(_od_inproj_kernel, use_rope=use_rope),
        out_shape=(jax.ShapeDtypeStruct((H_C, n, HEAD_PAD), BF16),
                   jax.ShapeDtypeStruct((n, KV_RANK), F32),
                   jax.ShapeDtypeStruct((n, ROPE_C), F32)),
        grid=(n // TM,),
        in_specs=in_specs,
        out_specs=(pl.BlockSpec((H_C, TM, HEAD_PAD), lambda i: (0, i, 0)),
                   pl.BlockSpec((TM, KV_RANK), lambda i: (i, 0)),
                   pl.BlockSpec((TM, ROPE_C), lambda i: (i, 0))),
        compiler_params=_cparams(1),
        name="od_inproj",
    )(*args)


def _kv_up_kernel(ckv_ref, kr_ref, wk_ref, ek_ref, wv_ref, k_ref, v_ref):
    ckv = ckv_ref[...].astype(BF16)
    k = (_dot(ckv, wk_ref[...]) + _dot(kr_ref[...].astype(BF16), ek_ref[...])).astype(BF16)
    v_t = lax.dot_general(wv_ref[...], ckv, (((1,), (1,)), ((), ())), preferred_element_type=F32)
    for h in range(H_C):
        k_ref[h] = k[:, h * HEAD_PAD:(h + 1) * HEAD_PAD]
    v_ref[...] = v_t.astype(BF16)


def _kv_up(ckv_all, kr_all, wk, ek, wv_t):
    m = ckv_all.shape[0]
    return pl.pallas_call(
        _kv_up_kernel,
        out_shape=(jax.ShapeDtypeStruct((H_C, m, HEAD_PAD), BF16),
                   jax.ShapeDtypeStruct((H_C * V_C, m), BF16)),
        grid=(m // TM,),
        in_specs=[pl.BlockSpec((TM, KV_RANK), lambda i: (i, 0)), pl.BlockSpec((TM, ROPE_C), lambda i: (i, 0)),
                  _const_spec((KV_RANK, H_C * HEAD_PAD)), _const_spec((ROPE_C, H_C * HEAD_PAD)),
                  _const_spec((H_C * V_C, KV_RANK))],
        out_specs=(pl.BlockSpec((H_C, TM, HEAD_PAD), lambda i: (0, i, 0)),
                   pl.BlockSpec((H_C * V_C, TM), lambda i: (0, i))),
        compiler_params=_cparams(1),
        name="kv_up",
    )(ckv_all, kr_all, wk, ek, wv_t)


def _attn_kernel(q_ref, k_ref, v_ref, o_ref, sa_ref, sb_ref):
    def scores(h, s_ref):
        s_ref[...] = lax.dot_general(k_ref[h], q_ref[h], (((1,), (1,)), ((), ())), preferred_element_type=F32)

    def over_keys(x, reduce):
        slabs = x.reshape(x.shape[0] // LANES, LANES, x.shape[1])
        return reduce(reduce(slabs, axis=0), axis=0, keepdims=True)

    def weighted_values(h, s_ref):
        s_t = s_ref[...]
        p_t = jnp.exp2(s_t - over_keys(s_t, jnp.max))
        l = over_keys(p_t, jnp.sum)
        v_t = v_ref[h * V_C:(h + 1) * V_C, :]
        return _dot(v_t, p_t.astype(BF16)) / l

    scores(0, sa_ref)
    for hp in range(H_C // 2):
        h = 2 * hp
        scores(h + 1, sb_ref)
        o_a = weighted_values(h, sa_ref)
        if h + 2 < H_C:
            scores(h + 2, sa_ref)
        o_b = weighted_values(h + 1, sb_ref)
        o_ref[hp] = jnp.concatenate([o_a, o_b], axis=0).T.astype(BF16)


def _attention(q, k, v_t, n_batch, seq_len, kv_len):
    tq = min(seq_len, TQ_MAX)
    per = seq_len // tq
    return pl.pallas_call(
        _attn_kernel,
        out_shape=jax.ShapeDtypeStruct((H_C // 2, q.shape[1], LANES), BF16),
        grid=(n_batch, per),
        in_specs=[pl.BlockSpec((H_C, tq, HEAD_PAD), lambda b, i: (0, b * per + i, 0)),
                  pl.BlockSpec((H_C, kv_len, HEAD_PAD), lambda b, i: (0, b, 0)),
                  pl.BlockSpec((H_C * V_C, kv_len), lambda b, i: (0, b))],
        out_specs=pl.BlockSpec((H_C // 2, tq, LANES), lambda b, i: (0, b * per + i, 0)),
        scratch_shapes=[pltpu.VMEM((kv_len, tq), F32)] * 2,
        compiler_params=_cparams(2),
        name="attention",
    )(q, k, v_t)


TOKEN_ROWS = D_MODEL // LANES


def _store_token_rows(ref, value, index=()):
    for s in range(TOKEN_ROWS):
        ref[index + (pl.ds(s, value.shape[0], stride=TOKEN_ROWS), slice(None))] = value[:, s * LANES:(s + 1) * LANES]


def _load_token_rows(ref, rows, index=()):
    parts = [ref[index + (pl.ds(s, rows, stride=TOKEN_ROWS), slice(None))] for s in range(TOKEN_ROWS)]
    return jnp.concatenate(parts, axis=-1)


ROUTE_IDX0, ROUTE_IDX1, ROUTE_RANK0, ROUTE_RANK1, ROUTE_W0, ROUTE_W1 = range(6)
ROUTER_SPLIT = 2


def _od_out_router_kernel(oa_ref, ob_ref, xa_ref, xb_ref, wo_ref, mod_ref, g_ref, wr_ref,
                          x3_ref, h2_ref, route_ref, cnt_ref, cnt_scr, *, tiles_a):
    i = pl.program_id(0)

    @pl.when(i == 0)
    def _():
        cnt_scr[...] = jnp.zeros(cnt_scr.shape, F32)

    from_a = i < tiles_a
    w_hi, w_lo = wr_ref[0], wr_ref[1]
    tm = x3_ref.shape[0] // ROUTER_SPLIT
    lane = lax.broadcasted_iota(jnp.int32, (tm, LANES), 1)
    lane_f = lane.astype(F32)
    neg = jnp.float32(-jnp.inf)
    ri = lax.broadcasted_iota(jnp.int32, (tm, tm), 0)
    ci = lax.broadcasted_iota(jnp.int32, (tm, tm), 1)
    before = jnp.where(ci < ri, 1.0, 0.0).astype(BF16)

    def rows_part(r0, seen):
        rs = pl.ds(r0, tm)
        o = jnp.concatenate([jnp.where(from_a, oa_ref[hp, rs, :], ob_ref[hp, rs, :]) for hp in range(H_C // 2)],
                            axis=-1)
        x3 = jnp.where(from_a, xa_ref[rs, :], xb_ref[rs, :]) + mod_ref[0, 2:3, :] * _dot(o, wo_ref[...])
        x3_ref[rs, :] = x3
        h2 = _modulated_norm(x3, g_ref[...], mod_ref[0, 4:5, :], mod_ref[0, 3:4, :])
        _store_token_rows(h2_ref.at[pl.ds(r0 * TOKEN_ROWS, tm * TOKEN_ROWS)], h2)
        h_hi = h2.astype(BF16)
        h_lo = (h2 - h_hi.astype(F32)).astype(BF16)
        logits = _dot(h_hi, w_hi) + (_dot(h_lo, w_hi) + _dot(h_hi, w_lo))
        logits = jnp.where(lane < N_EXPERTS, logits, neg)
        m0 = jnp.max(logits, axis=-1, keepdims=True)
        i0 = jnp.min(jnp.where(logits == m0, lane_f, float(LANES)), axis=-1, keepdims=True)
        rest = jnp.where(lane_f == i0, neg, logits)
        m1 = jnp.max(rest, axis=-1, keepdims=True)
        i1 = jnp.min(jnp.where(rest == m1, lane_f, float(LANES)), axis=-1, keepdims=True)
        e = jnp.exp(m1 - m0)
        w0 = 1.0 / (1.0 + e)
        w1 = e * w0
        oh0 = jnp.where(lane_f == i0, 1.0, 0.0).astype(F32)
        oh1 = jnp.where(lane_f == i1, 1.0, 0.0).astype(F32)
        sel = oh0 + oh1
        prior = _dot(before, sel.astype(BF16)) + seen
        rank0 = jnp.sum(oh0 * prior, axis=-1, keepdims=True)
        rank1 = jnp.sum(oh1 * prior, axis=-1, keepdims=True)
        route = jnp.where(lane == ROUTE_IDX0, i0, 0.0)
        route = jnp.where(lane == ROUTE_IDX1, i1, route)
        route = jnp.where(lane == ROUTE_RANK0, rank0, route)
        route = jnp.where(lane == ROUTE_RANK1, rank1, route)
        route = jnp.where(lane == ROUTE_W0, w0, route)
        route = jnp.where(lane == ROUTE_W1, w1, route)
        route_ref[rs, :] = route
        return seen + jnp.sum(sel, axis=0, keepdims=True)

    seen = cnt_scr[0:1, :]
    for part in range(ROUTER_SPLIT):
        seen = rows_part(part * tm, seen)
    cnt_scr[...] = jnp.broadcast_to(seen, cnt_scr.shape)
    cnt_ref[...] = cnt_scr[...]


def _od_out_router(o_a, o_b, x_a, x_b, wo, mod_l, cond_of_tile, gain, wr):
    n_a, n_b = x_a.shape[0], x_b.shape[0]
    n = n_a + n_b
    tiles_a = n_a // TM
    row = lambda i: (i, 0)
    a_tile = lambda i: jnp.minimum(i, tiles_a - 1)
    b_tile = lambda i: jnp.maximum(i - tiles_a, 0)
    return pl.pallas_call(
        functools.partial(_od_out_router_kernel, tiles_a=tiles_a),
        out_shape=(jax.ShapeDtypeStruct((n, D_MODEL), F32), jax.ShapeDtypeStruct((n * TOKEN_ROWS, LANES), F32),
                   jax.ShapeDtypeStruct((n, LANES), F32), jax.ShapeDtypeStruct((8, LANES), F32)),
        grid=(n // TM,),
        in_specs=[pl.BlockSpec((H_C // 2, TM, LANES), lambda i: (0, a_tile(i), 0)),
                  pl.BlockSpec((H_C // 2, TM, LANES), lambda i: (0, b_tile(i), 0)),
                  pl.BlockSpec((TM, D_MODEL), lambda i: (a_tile(i), 0)),
                  pl.BlockSpec((TM, D_MODEL), lambda i: (b_tile(i), 0)),
                  _const_spec((D_MODEL, D_MODEL)),
                  pl.BlockSpec((1, 6, D_MODEL), lambda i: (cond_of_tile(i), 0, 0)),
                  _const_spec((1, D_MODEL)), _const_spec((2, D_MODEL, LANES))],
        out_specs=(pl.BlockSpec((TM, D_MODEL), row), pl.BlockSpec((TM * TOKEN_ROWS, LANES), row),
                   pl.BlockSpec((TM, LANES), row), pl.BlockSpec((8, LANES), lambda i: (0, 0))),
        scratch_shapes=[pltpu.VMEM((8, LANES), F32)],
        compiler_params=_cparams(1),
        name="od_out_router",
    )(o_a, o_b, x_a, x_b, wo, mod_l, gain, wr)


def _token_copy(src, s8, dst, d8, sem):
    return pltpu.make_async_copy(src.at[pl.ds(pl.multiple_of(s8, TOKEN_ROWS), TOKEN_ROWS)],
                                 dst.at[pl.ds(pl.multiple_of(d8, TOKEN_ROWS), TOKEN_ROWS)], sem)


ROW_DMA_UNROLL = 8
N_FF_CHUNKS = D_FF_E // FF_CHUNK_E
GATHERS_PER_STEP = TM_E // N_FF_CHUNKS
ROW_DMA_PRIORITY = 1


def _experts_kernel(te_ref, nt_ref, src0_ref, src1_ref, dst_ref, dstp_ref, h_hbm, wg_ref, wu_ref, wd_ref, y_hbm,
                    h_buf, y_buf, h_scr, acc_scr, sem_in, sem_out):
    i, c = pl.program_id(0), pl.program_id(1)
    last_c = pl.num_programs(1) - 1
    n_tiles = nt_ref[0]
    tile_rows = TM_E * TOKEN_ROWS

    def gather(buf, src_ref):
        def body(r, carry):
            _token_copy(h_hbm, src_ref[r], h_buf.at[buf], r * TOKEN_ROWS, sem_in.at[buf]).start(ROW_DMA_PRIORITY)
            return carry
        lax.fori_loop(0, TM_E, body, 0, unroll=ROW_DMA_UNROLL)

    def next_token(r):
        nbuf = (i + 1) % 2
        _token_copy(h_hbm, src1_ref[r], h_buf.at[nbuf], r * TOKEN_ROWS, sem_in.at[nbuf]).start(ROW_DMA_PRIORITY)

    def prev_token(r):
        _token_copy(y_buf, r * TOKEN_ROWS, y_hbm, dstp_ref[r], sem_out).start(ROW_DMA_PRIORITY)

    def tile_copy(src, dst, sem):
        return pltpu.make_async_copy(src.at[pl.ds(0, tile_rows)], dst.at[pl.ds(0, tile_rows)], sem)

    @pl.when(i < n_tiles)
    def _():
        @pl.when(c == 0)
        def _():
            @pl.when(i == 0)
            def _():
                gather(0, src0_ref)
                y_buf[...] = jnp.zeros(y_buf.shape, F32)
                spare = pltpu.make_async_copy(y_buf, y_hbm.at[pl.ds(y_hbm.shape[0] - tile_rows, tile_rows)], sem_out)
                spare.start()
                spare.wait()

            @pl.when(i + 1 < n_tiles)
            def _():
                for r in range(N_FF_CHUNKS * GATHERS_PER_STEP, TM_E):
                    next_token(r)

            @pl.when(i > 0)
            def _():
                for r in range(N_FF_CHUNKS * GATHERS_PER_STEP, TM_E):
                    prev_token(r)

            buf = i % 2
            tile_copy(h_hbm, h_buf.at[buf], sem_in.at[buf]).wait()
            h_scr[...] = _load_token_rows(h_buf, TM_E, (buf,)).astype(BF16)

        def hidden_chunk():
            h = h_scr[...]
            g = _dot(h, wg_ref[0].astype(BF16))
            u = _dot(h, wu_ref[0].astype(BF16))
            return _dot((_silu(g) * u).astype(BF16), wd_ref[0].astype(BF16))

        def accumulate(part):
            @pl.when(c == 0)
            def _():
                acc_scr[...] = part

            @pl.when((c > 0) & (c < last_c))
            def _():
                acc_scr[...] += part

            @pl.when(c == last_c)
            def _():
                @pl.when(i > 0)
                def _():
                    tile_copy(y_buf, y_hbm, sem_out).wait()
                _store_token_rows(y_buf, acc_scr[...] + part)

                @pl.when(i == n_tiles - 1)
                def _():
                    def body(r, carry):
                        _token_copy(y_buf, r * TOKEN_ROWS, y_hbm, dst_ref[r], sem_out).start(ROW_DMA_PRIORITY)
                        return carry
                    lax.fori_loop(0, TM_E, body, 0, unroll=ROW_DMA_UNROLL)
                    tile_copy(y_buf, y_hbm, sem_out).wait()

        def step(has_next, has_prev):
            for j in range(GATHERS_PER_STEP):
                if has_next:
                    next_token(c * GATHERS_PER_STEP + j)
                if has_prev:
                    prev_token(c * GATHERS_PER_STEP + j)
            accumulate(hidden_chunk())

        has_next, has_prev = i + 1 < n_tiles, i > 0
        for want_next in (True, False):
            for want_prev in (True, False):
                pl.when((has_next == want_next) & (has_prev == want_prev))(
                    functools.partial(step, want_next, want_prev))


def _experts(tile_expert, n_tiles, slot_src, slot_dst, h_all, wgu, wd, n_out_tokens):
    rows = slot_src.shape[0]
    t = rows // TM_E
    n_c = D_FF_E // FF_CHUNK_E
    assert n_c > 1
    live = lambda i, nt: jnp.maximum(jnp.minimum(i, nt[0] - 1), 0)
    smem_rows = lambda f: pl.BlockSpec((TM_E,), f, memory_space=pltpu.SMEM)
    grid_spec = pltpu.PrefetchScalarGridSpec(
        num_scalar_prefetch=2,
        grid=(t, n_c),
        in_specs=[smem_rows(lambda i, c, te, nt: (0,)),
                  smem_rows(lambda i, c, te, nt: (jnp.minimum(i + 1, t - 1),)),
                  smem_rows(lambda i, c, te, nt: (i,)),
                  smem_rows(lambda i, c, te, nt: (jnp.maximum(i - 1, 0),)),
                  pl.BlockSpec(memory_space=pl.ANY),
                  pl.BlockSpec((1, D_MODEL, FF_CHUNK_E), lambda i, c, te, nt: (te[live(i, nt)], 0, c)),
                  pl.BlockSpec((1, D_MODEL, FF_CHUNK_E), lambda i, c, te, nt: (te[live(i, nt)], 0, n_c + c)),
                  pl.BlockSpec((1, FF_CHUNK_E, D_MODEL), lambda i, c, te, nt: (te[live(i, nt)], c, 0))],
        out_specs=pl.BlockSpec(memory_space=pl.ANY),
        scratch_shapes=[pltpu.VMEM((2, TM_E * TOKEN_ROWS, LANES), F32), pltpu.VMEM((TM_E * TOKEN_ROWS, LANES), F32),
                        pltpu.VMEM((TM_E, D_MODEL), BF16), pltpu.VMEM((TM_E, D_MODEL), F32),
                        pltpu.SemaphoreType.DMA((2,)), pltpu.SemaphoreType.DMA(())],
    )
    return pl.pallas_call(
        _experts_kernel,
        out_shape=jax.ShapeDtypeStruct(((n_out_tokens + TM_E) * TOKEN_ROWS, LANES), F32),
        grid_spec=grid_spec,
        compiler_params=pltpu.CompilerParams(dimension_semantics=("arbitrary", "arbitrary"),
                                             vmem_limit_bytes=VMEM_LIMIT, has_side_effects=True),
        name="experts",
    )(tile_expert, n_tiles, slot_src, slot_src, slot_dst, slot_dst, h_all, wgu, wgu, wd)


def _combine_kernel(x_ref, ya_ref, yb_ref, route_ref, mod_ref, g_ref, o_ref):
    w0 = route_ref[:, ROUTE_W0:ROUTE_W0 + 1]
    w1 = route_ref[:, ROUTE_W1:ROUTE_W1 + 1]
    rows = x_ref.shape[0]
    x = x_ref[...] + mod_ref[0, 5:6, :] * (w0 * _load_token_rows(ya_ref, rows) + w1 * _load_token_rows(yb_ref, rows))
    o_ref[...] = x * lax.rsqrt(jnp.mean(x * x, axis=-1, keepdims=True) + EPS) * g_ref[...]


def _combine(x3, y2, route, first_row, n_rows, n_tok, mod_l, cond_of_tile, final_g):
    nt = n_rows // TM
    blk0 = first_row // TM
    blk1 = (n_tok + first_row) // TM
    return pl.pallas_call(
        _combine_kernel,
        out_shape=jax.ShapeDtypeStruct((n_rows, D_MODEL), F32),
        grid=(nt,),
        in_specs=[pl.BlockSpec((TM, D_MODEL), lambda i: (blk0 + i, 0)),
                  pl.BlockSpec((TM * TOKEN_ROWS, LANES), lambda i: (blk0 + i, 0)),
                  pl.BlockSpec((TM * TOKEN_ROWS, LANES), lambda i: (blk1 + i, 0)),
                  pl.BlockSpec((TM, LANES), lambda i: (blk0 + i, 0)),
                  pl.BlockSpec((1, 6, D_MODEL), lambda i: (cond_of_tile(blk0 + i), 0, 0)),
                  _const_spec((1, D_MODEL))],
        out_specs=pl.BlockSpec((TM, D_MODEL), lambda i: (i, 0)),
        compiler_params=_cparams(1),
        name="combine",
    )(x3, y2, y2, route, mod_l, final_g)


def _even_params(ev_w_in, ev_conv_a_w, ev_conv_a_b, ev_a_log, ev_dt_bias, ev_d_skip, ev_gnorm,
                 ev_conv_b_w, ev_conv_b_b, ev_lru_w, ev_lru_b, ev_lru_lam, ev_w_out, ev_ffn_gu, ev_ffn_down):
    d_a = H_A * P_A
    xbc = d_a + 2 * G_A * N_A
    o_xbc, o_dt, o_gate, o_xb = d_a, d_a + xbc, d_a + xbc + 2 * H_A, d_a + xbc + 2 * H_A + D_MODEL
    w = ev_w_in
    w_main = jnp.concatenate([w[:, :d_a], w[:, o_xb:], w[:, o_gate:o_xb], w[:, o_xbc:o_dt]], axis=1).astype(BF16)
    w_dt = jnp.pad(w[:, o_dt:o_gate], ((0, 0), (0, LANES - 2 * H_A))).astype(BF16)
    zeros_w = jnp.zeros((CONV_K, D_MODEL), F32)
    cw = jnp.concatenate([zeros_w, ev_conv_b_w, zeros_w, ev_conv_a_w], axis=1)
    zeros_b = jnp.zeros((D_MODEL,), F32)
    cb = jnp.concatenate([zeros_b, ev_conv_b_b, zeros_b, ev_conv_a_b])[None, :]
    dtb = jnp.pad(ev_dt_bias.reshape(-1), (0, LANES - 2 * H_A))[None, :]
    alog = jnp.pad(ev_a_log.reshape(-1), (0, LANES - 2 * H_A))[None, :]
    j = jnp.arange(2 * LANES) % LANES
    c = jnp.arange(d_a) // P_A
    e2 = jnp.stack([(j[:, None] == (d * H_A + c)[None, :]) for d in range(2)]).astype(BF16)
    wl = jnp.transpose(ev_lru_w, (2, 3, 0, 1, 4)).reshape(H_B, BW_B, 4 * BW_B).astype(BF16)
    lb = jnp.transpose(ev_lru_b.reshape(2, 2, H_B, BW_B), (2, 0, 1, 3)).reshape(H_B, 1, 4 * BW_B)
    lam = jnp.transpose(ev_lru_lam.reshape(2, H_B, BW_B), (1, 0, 2))
    dsk = jnp.repeat(ev_d_skip, P_A)[None, :]
    return dict(w_main=w_main, w_dt=w_dt, cw=cw, cb=cb, dtb=dtb, alog=alog, e2=e2, wl=wl, lb=lb, lam=lam,
                dsk=dsk, gn=ev_gnorm[None, :], w_out=ev_w_out.astype(BF16),
                wgu=ev_ffn_gu.astype(BF16), wd=ev_ffn_down.astype(BF16))


def _rope_partner(w, lo):
    blk = w[:, lo:lo + ROPE_C].reshape(w.shape[0], 2, 2, N_FREQ)
    return jnp.flip(blk, axis=2).reshape(w.shape[0], ROPE_C)


def _odd_params(od_w_in, od_q_norm, od_w_q_up, od_kv_norm, od_w_kv_up, od_w_o, od_router):
    kr_lo = Q_RANK + KV_RANK
    wi = jnp.zeros((D_MODEL, OD_IN_COLS), F32)
    wi = wi.at[:, :kr_lo + ROPE_C].set(od_w_in)
    wi = wi.at[:, 768:768 + ROPE_C].set(_rope_partner(od_w_in, kr_lo))
    wq3 = od_w_q_up.reshape(Q_RANK, H_C, NOPE_C + ROPE_C)
    pad = HEAD_PAD - NOPE_C - ROPE_C
    wq = jnp.pad(wq3, ((0, 0), (0, 0), (0, pad))).reshape(Q_RANK, H_C * HEAD_PAD)
    partner = jnp.flip(wq3[:, :, NOPE_C:].reshape(Q_RANK, H_C, 2, 2, N_FREQ), axis=3).reshape(Q_RANK, H_C, ROPE_C)
    wqs = jnp.pad(partner, ((0, 0), (0, 0), (NOPE_C, pad))).reshape(Q_RANK, H_C * HEAD_PAD)
    wkv3 = od_w_kv_up.reshape(KV_RANK, H_C, NOPE_C + V_C)
    wk = jnp.pad(wkv3[:, :, :NOPE_C], ((0, 0), (0, 0), (0, HEAD_PAD - NOPE_C))).reshape(KV_RANK, H_C * HEAD_PAD)
    wv_t = wkv3[:, :, NOPE_C:].reshape(KV_RANK, H_C * V_C).T
    ek_head = jnp.pad(jnp.eye(ROPE_C, dtype=F32), ((0, 0), (NOPE_C, pad)))
    ek = jnp.tile(ek_head, (1, H_C))
    wr_f = jnp.pad(od_router, ((0, 0), (0, LANES - N_EXPERTS)))
    wr_hi = wr_f.astype(BF16)
    wr = jnp.stack([wr_hi, (wr_f - wr_hi.astype(F32)).astype(BF16)])
    return dict(wi=wi.astype(BF16), qn=od_q_norm[None, :], kvn=od_kv_norm[None, :], wq=wq.astype(BF16),
                wqs=wqs.astype(BF16), wk=wk.astype(BF16), ek=ek.astype(BF16), wv_t=wv_t.astype(BF16),
                wo=od_w_o.astype(BF16), wr=wr)


def _rope_tables(n_tokens):
    rows = n_tokens // GRID_W
    row = jnp.repeat(jnp.arange(rows), GRID_W).astype(F32)
    col = jnp.tile(jnp.arange(GRID_W), rows).astype(F32)
    inv = ROPE_BASE ** (-jnp.arange(N_FREQ, dtype=F32) / N_FREQ)
    ang_r, ang_c = row[:, None] * inv, col[:, None] * inv
    cos_k = jnp.concatenate([jnp.cos(ang_r)] * 2 + [jnp.cos(ang_c)] * 2, axis=1)
    sin_k = jnp.concatenate([-jnp.sin(ang_r), jnp.sin(ang_r), -jnp.sin(ang_c), jnp.sin(ang_c)], axis=1)
    pad = HEAD_PAD - NOPE_C - ROPE_C
    cos_q = jnp.pad(cos_k, ((0, 0), (NOPE_C, pad)), constant_values=1.0)
    sin_q = jnp.pad(sin_k, ((0, 0), (NOPE_C, pad)))
    return cos_q, sin_q, cos_k, sin_k


def _layer0(x2d, mod_l, cond_row, norm_g0, ev, n_batch, seq_len, h0_ssd, h0_lru, is_ctx):
    per_block = max(ROW_BLOCK // seq_len, 1)
    blocks_per_seq = max(seq_len // ROW_BLOCK, 1)
    cond_of_block = (lambda i: 0) if is_ctx else (lambda i: cond_row(i // blocks_per_seq))
    tiles_per_seq = seq_len // TM if seq_len >= TM else 1
    cond_of_tile = (lambda i: 0) if is_ctx else (lambda i: cond_row(i // tiles_per_seq))
    p, dt = _ev_inproj(x2d, mod_l, cond_of_block, norm_g0[0:1], ev["w_main"], ev["w_dt"], ev["cw"], ev["cb"],
                       ev["dtb"], seq_len)
    ssd_out = _ssd(p, dt, ev["alog"], ev["e2"], h0_ssd, n_batch, seq_len, emit_final=is_ctx)
    lru_out = _lru(p, ev["wl"], ev["lb"], ev["lam"], h0_lru, chain=not is_ctx, emit_final=is_ctx)
    x1 = _ev_out(x2d, ssd_out[0], ssd_out[1], p, lru_out[0], ev["w_out"], ev["dsk"], ev["gn"], mod_l, cond_of_tile)
    x2 = _ffn(x1, mod_l, cond_of_tile, norm_g0[1:2], ev["wgu"], ev["wd"])
    finals = None
    if is_ctx:
        hp = H_A * P_A
        s_ssd = ssd_out[2].reshape(n_batch, 1, 2, H_A, P_A, N_A)
        fin = lru_out[1]
        s_lru = jnp.transpose(fin, (0, 3, 2, 1, 4)).reshape(n_batch, 1, 2, H_B * BW_B)
        finals = (s_ssd, s_lru)
        del hp, per_block
    return x2, finals, cond_of_tile


def _layer1_pre(x2d, mod_l, cond_of_tile, norm_g1, od, n_batch, seq_len, rope, cache):
    q, ckv, kr = _od_inproj(x2d, mod_l, cond_of_tile, norm_g1[0:1], od["wi"], od["qn"], od["kvn"], od["wq"],
                            rope, seq_len)
    if cache is None:
        ckv_all, kr_all, kv_len = ckv, kr, seq_len
    else:
        c_ckv, c_kr = cache
        past = c_ckv.shape[1]
        kv_len = past + seq_len
        ckv_all = jnp.concatenate([c_ckv, ckv.reshape(n_batch, seq_len, KV_RANK)], axis=1).reshape(-1, KV_RANK)
        kr_all = jnp.concatenate([c_kr, kr.reshape(n_batch, seq_len, ROPE_C)], axis=1).reshape(-1, ROPE_C)
    k, v_t = _kv_up(ckv_all, kr_all, od["wk"], od["ek"], od["wv_t"])
    o = _attention(q, k, v_t, n_batch, seq_len, kv_len)
    return o, ckv, kr


def kernel(x_prompt, x_sample, state_ssd, state_lru, cache_ckv, cache_krope, c, c_ctx, mod_w, mod_b, norm_g, final_g, ev_w_in, ev_conv_a_w, ev_conv_a_b, ev_a_log, ev_dt_bias, ev_d_skip, ev_gnorm, ev_conv_b_w, ev_conv_b_b, ev_lru_w, ev_lru_b, ev_lru_lam, ev_w_out, ev_ffn_gu, ev_ffn_down, od_w_in, od_q_norm, od_w_q_up, od_kv_norm, od_w_kv_up, od_w_o, od_router, od_moe_gu, od_moe_down):
    batch, seq, _ = x_prompt.shape
    dbatch, dseq, _ = x_sample.shape
    n_ctx, n_dec = batch * seq, dbatch * dseq

    cond = jnp.concatenate([c_ctx[None, :], c, jnp.zeros((16 - 1 - dbatch, D_MODEL), F32)], axis=0)
    mod = _modulation(cond, mod_w, mod_b)
    ev = _even_params(ev_w_in[0], ev_conv_a_w[0], ev_conv_a_b[0], ev_a_log[0], ev_dt_bias[0], ev_d_skip[0],
                      ev_gnorm[0], ev_conv_b_w[0], ev_conv_b_b[0], ev_lru_w[0], ev_lru_b[0], ev_lru_lam[0],
                      ev_w_out[0], ev_ffn_gu[0], ev_ffn_down[0])
    od = _odd_params(od_w_in[0], od_q_norm[0], od_w_q_up[0], od_kv_norm[0], od_w_kv_up[0], od_w_o[0], od_router[0])
    wgu_e = od_moe_gu[0]
    wd_e = od_moe_down[0]
    dec_row = lambda b: 1 + b

    xc, finals, tile_c = _layer0(x_prompt.reshape(n_ctx, D_MODEL), mod[0], None, norm_g[0], ev, batch, seq,
                                 None, None, True)
    h0_ssd = state_ssd[:, 0].reshape(dbatch, 2, H_A * P_A, N_A)
    h0_lru = jnp.transpose(state_lru[:, 0].reshape(dbatch, 2, H_B, BW_B), (0, 2, 1, 3))
    xd, _, tile_d = _layer0(x_sample.reshape(n_dec, D_MODEL), mod[0], dec_row, norm_g[0], ev, dbatch, dseq,
                            h0_ssd, h0_lru, False)
    new_state_ssd, new_state_lru = finals

    rope = (od["wqs"],) + _rope_tables(dseq)
    o_c, ckv_c, kr_c = _layer1_pre(xc, mod[1], tile_c, norm_g[1], od, batch, seq, None, None)
    o_d, _, _ = _layer1_pre(xd, mod[1], tile_d, norm_g[1], od, dbatch, dseq, rope,
                            (cache_ckv[:, 0], cache_krope[:, 0]))
    new_cache_ckv = ckv_c.reshape(batch, 1, seq, KV_RANK)
    new_cache_krope = kr_c.reshape(batch, 1, seq, ROPE_C)

    tiles_c = n_ctx // TM
    tile_all = lambda i: jnp.where(i < tiles_c, 0, tile_d(jnp.maximum(i - tiles_c, 0)))
    x3, h2, route, counts = _od_out_router(o_c, o_d, xc, xd, od["wo"], mod[1], tile_all, norm_g[1][1:2], od["wr"])
    y2 = _routed_experts(h2, route, counts, wgu_e, wd_e)
    n_tok = n_ctx + n_dec
    fg = final_g[None, :]
    y_prompt = _combine(x3, y2, route, 0, n_ctx, n_tok, mod[1], tile_all, fg).reshape(batch, seq, D_MODEL)
    y_sample = _combine(x3, y2, route, n_ctx, n_dec, n_tok, mod[1], tile_all, fg).reshape(dbatch, dseq, D_MODEL)
    return (y_prompt, y_sample, new_state_ssd, new_state_lru, new_cache_ckv, new_cache_krope)


def _routed_experts(h2, route, counts, wgu_e, wd_e):
    n_tok = route.shape[0]
    n_slots = (2 * n_tok // TM_E + N_EXPERTS) * TM_E
    cnt_e = counts[0, :N_EXPERTS].astype(jnp.int32)
    tiles_e = (cnt_e + TM_E - 1) // TM_E
    tile_end = jnp.cumsum(tiles_e)
    slot_start = (tile_end - tiles_e) * TM_E
    n_tiles = tile_end[-1:].astype(jnp.int32)
    tile_id = jnp.arange(n_slots // TM_E, dtype=jnp.int32)
    tile_expert = jnp.minimum(jnp.sum((tile_end[None, :] <= tile_id[:, None]).astype(jnp.int32), axis=1),
                              N_EXPERTS - 1)
    idx = route[:, ROUTE_IDX0:ROUTE_IDX1 + 1].astype(jnp.int32)
    rank = route[:, ROUTE_RANK0:ROUTE_RANK1 + 1].astype(jnp.int32)
    slot = slot_start[idx] + rank
    out_tok = jnp.arange(n_tok, dtype=jnp.int32)[:, None] + jnp.array([0, n_tok], jnp.int32)[None, :]
    spare = 2 * n_tok + jnp.arange(n_slots, dtype=jnp.int32) % TM_E
    slot_dst = spare.at[slot.reshape(-1)].set(out_tok.reshape(-1))
    slot_src = jnp.where(slot_dst >= 2 * n_tok, 0, jnp.where(slot_dst >= n_tok, slot_dst - n_tok, slot_dst))
    return _experts(tile_expert, n_tiles, slot_src * TOKEN_ROWS, slot_dst * TOKEN_ROWS, h2, wgu_e, wd_e, 2 * n_tok)
```
